```python
import math
import jax
import jax.numpy as jnp
from jax import lax
import numpy as np

D_MODEL = 1024
BATCH = 16
SEQ = 4096
DEPTH = 1
DEC_BATCH = 32
DEC_SEQ = 64
PAST_LEN = 2048

CHUNK = 64
QBLK = 64
SSD_EXPAND = 2
D_INNER = SSD_EXPAND * D_MODEL
SSD_HEAD_DIM = 64
N_SSD_HEADS = D_INNER // SSD_HEAD_DIM
N_GROUPS = 4
D_STATE = 128
CONV_W = 4
CONV_DIM = D_INNER + 2 * N_GROUPS * D_STATE
HEAD_DIM = 64
N_HEADS = D_MODEL // HEAD_DIM
N_KV = 4
KV_GROUP = N_HEADS // N_KV
D_ATT = N_HEADS * HEAD_DIM
N_IDX_HEADS = 8
D_IDX = 64
TOPK_MAX = 256
EPS = 1e-6

PROJ_SIZES = (D_INNER, CONV_DIM, N_SSD_HEADS, D_ATT, N_KV * HEAD_DIM, N_KV * HEAD_DIM, D_ATT,
              N_IDX_HEADS * D_IDX, D_IDX, N_IDX_HEADS, D_MODEL, D_MODEL)
D_PROJ = sum(PROJ_SIZES)

kernel_name = 'hybrid_ssd_dsa_stream_step'


def rms_norm(x, g):
    xf = x.astype(jnp.float32)
    y = xf * lax.rsqrt(jnp.mean(xf * xf, axis=-1, keepdims=True) + EPS)
    return (y * g.astype(jnp.float32)).astype(x.dtype)


def split_proj(p):
    parts, off = [], 0
    for n in PROJ_SIZES:
        parts.append(p[..., off:off + n])
        off += n
    return parts


def segsum(a):
    T = a.shape[-1]
    low_strict = jnp.tril(jnp.ones((T, T), dtype=bool), -1)
    low = jnp.tril(jnp.ones((T, T), dtype=bool), 0)
    cs = jnp.cumsum(jnp.where(low_strict, a[..., :, None], 0.0), axis=-2)
    return jnp.where(low, cs, -jnp.inf)


def ssd_scan(x, dt, a, bm, cm, h0):
    b, L = x.shape[:2]
    cl = min(CHUNK, L)
    nc = L // cl
    E = N_SSD_HEADS // N_GROUPS
    x_c = jnp.moveaxis((x * dt[..., None]).reshape(b, nc, cl, N_GROUPS, E, SSD_HEAD_DIM), 1, 0)
    la_c = jnp.transpose((dt * a).reshape(b, nc, cl, N_GROUPS, E), (1, 0, 3, 4, 2))
    b_c = jnp.moveaxis(bm.reshape(b, nc, cl, N_GROUPS, D_STATE), 1, 0)
    c_c = jnp.moveaxis(cm.reshape(b, nc, cl, N_GROUPS, D_STATE), 1, 0)

    def step(h, inp):
        xk, lak, bk, ck = inp
        acum = jnp.cumsum(lak, axis=-1)
        w = jnp.einsum('blgn,bsgn->bgls', ck, bk)[:, :, None] * jnp.exp(segsum(lak))
        y_diag = jnp.einsum('bgels,bsgep->blgep', w, xk)
        y_off = jnp.einsum('blgn,bgepn->blgep', ck, h) * jnp.exp(jnp.moveaxis(acum, -1, 1))[..., None]
        decay_end = jnp.exp(acum[..., -1:] - acum)
        st = jnp.einsum('blgn,bgel,blgep->bgepn', bk, decay_end, xk)
        h = jnp.exp(acum[..., -1])[..., None, None] * h + st
        return h, y_diag + y_off

    h0g = h0.reshape(b, N_GROUPS, E, SSD_HEAD_DIM, D_STATE)
    h_fin, y = lax.scan(step, h0g, (x_c, la_c, b_c, c_c))
    y = jnp.moveaxis(y, 0, 1).reshape(b, L, N_SSD_HEADS, SSD_HEAD_DIM)
    return y, h_fin.reshape(b, N_SSD_HEADS, SSD_HEAD_DIM, D_STATE)


def ssd_branch(z, xbc, dt_raw, conv_hist, h0, conv_w, conv_b, dt_bias, a_log, d_skip, ssd_norm_g):
    b, L, _ = xbc.shape
    f32 = jnp.float32
    xpad = jnp.concatenate([conv_hist.astype(xbc.dtype), xbc], axis=1)
    acc = conv_b
    for i in range(CONV_W):
        acc = acc + xpad[:, i:i + L] * conv_w[i]
    conv_new = xpad[:, L:]
    u = jax.nn.silu(acc).astype(f32)
    xs = u[..., :D_INNER].reshape(b, L, N_SSD_HEADS, SSD_HEAD_DIM)
    bm = u[..., D_INNER:D_INNER + N_GROUPS * D_STATE].reshape(b, L, N_GROUPS, D_STATE)
    cm = u[..., D_INNER + N_GROUPS * D_STATE:].reshape(b, L, N_GROUPS, D_STATE)
    dt = jax.nn.softplus(dt_raw.astype(f32) + dt_bias.astype(f32))
    a = -jnp.exp(a_log.astype(f32))
    y, h = ssd_scan(xs, dt, a, bm, cm, h0.astype(f32))
    y = (y + d_skip.astype(f32)[:, None] * xs).reshape(b, L, D_INNER)
    y = y * jax.nn.silu(z.astype(f32))
    y = rms_norm(y.reshape(b, L, N_GROUPS, D_INNER // N_GROUPS),
                 ssd_norm_g.reshape(N_GROUPS, D_INNER // N_GROUPS)).reshape(b, L, D_INNER)
    return y.astype(xbc.dtype), conv_new, h


def alibi_slopes():
    return jnp.exp2(-8.0 * jnp.arange(1, N_HEADS + 1, dtype=jnp.float32) / N_HEADS)


def dsa_block(q, qi, wi, qpos, k_all, v_all, ki_all, kpos, n_sel, slopes):
    b, tq = q.shape[:2]
    f32 = jnp.float32
    s_idx = jnp.einsum('bthd,bsd->bths', qi, ki_all).astype(f32) * (D_IDX ** -0.5)
    score = jnp.einsum('bths,bth->bts', jax.nn.relu(s_idx), wi)
    admiss = (kpos[None, :] // CHUNK) <= (qpos[:, None] // CHUNK)
    score = jnp.where(admiss[None], score, -jnp.inf)
    top_val, top_idx = lax.top_k(score, n_sel)
    ok = jnp.isfinite(top_val)
    bi = jnp.arange(b)[:, None, None]
    k_sel = k_all[bi, top_idx]
    v_sel = v_all[bi, top_idx]
    dist = jnp.abs(qpos[None, :, None] - kpos[top_idx]).astype(f32)
    qg = q.reshape(b, tq, N_KV, KV_GROUP, HEAD_DIM)
    logits = jnp.einsum('btkgd,btnkd->btkgn', qg, k_sel).astype(f32) * (HEAD_DIM ** -0.5)
    logits = logits - slopes[None, None, :, :, None] * dist[:, :, None, None, :]
    logits = jnp.where(ok[:, :, None, None, :], logits, -jnp.inf)
    p = jax.nn.softmax(logits, axis=-1).astype(v_all.dtype)
    o = jnp.einsum('btkgn,btnkd->btkgd', p, v_sel)
    return o.reshape(b, tq, D_ATT)


def dsa_attention(q, qi, wi, k_all, v_all, ki_all, past, n_sel):
    b, L = q.shape[:2]
    kpos = jnp.arange(k_all.shape[1])
    qb = min(QBLK, L)
    nblk = L // qb
    slopes = alibi_slopes().reshape(N_KV, KV_GROUP)

    def to_blocks(t):
        return jnp.moveaxis(t.reshape((b, nblk, qb) + t.shape[2:]), 1, 0)

    def blk(args):
        j, q_b, qi_b, wi_b = args
        qpos = past + j * qb + jnp.arange(qb)
        return dsa_block(q_b, qi_b, wi_b, qpos, k_all, v_all, ki_all, kpos, n_sel, slopes)

    out = lax.map(blk, (jnp.arange(nblk), to_blocks(q), to_blocks(qi), to_blocks(wi)))
    return jnp.moveaxis(out, 0, 1).reshape(b, L, D_ATT)


def mixer_layer(x, conv_hist, ssm_h0, k_past, v_past, ki_past, norm_g, w_in, conv_w, conv_b, dt_bias,
                a_log, d_skip, ssd_norm_g, q_norm_g, k_norm_g, idx_k_norm_g, w_ssd_o, w_att_o, w_out):
    b, L, _ = x.shape
    h = rms_norm(x, norm_g)
    z_s, xbc, dt_raw, q, k, v, z_a, qi, ki, wi, g_s, g_a = split_proj(h @ w_in)
    y_s, conv_new, ssm_new = ssd_branch(z_s, xbc, dt_raw, conv_hist, ssm_h0, conv_w, conv_b, dt_bias,
                                        a_log, d_skip, ssd_norm_g)
    q = rms_norm(q.reshape(b, L, N_HEADS, HEAD_DIM), q_norm_g)
    k = rms_norm(k.reshape(b, L, N_KV, HEAD_DIM), k_norm_g)
    v = v.reshape(b, L, N_KV, HEAD_DIM)
    qi = qi.reshape(b, L, N_IDX_HEADS, D_IDX)
    ki = rms_norm(ki, idx_k_norm_g)
    wi = wi.astype(jnp.float32) * (N_IDX_HEADS ** -0.5)
    if k_past is None:
        past, k_all, v_all, ki_all = 0, k, v, ki
    else:
        past = k_past.shape[1]
        k_all = jnp.concatenate([k_past.astype(k.dtype), k], axis=1)
        v_all = jnp.concatenate([v_past.astype(v.dtype), v], axis=1)
        ki_all = jnp.concatenate([ki_past.astype(ki.dtype), ki], axis=1)
    n_sel = min(TOPK_MAX, (past + L) // 4)
    o_a = dsa_attention(q, qi, wi, k_all, v_all, ki_all, past, n_sel)
    y_a = o_a * jax.nn.silu(z_a)
    merged = jax.nn.sigmoid(g_s) * (y_s @ w_ssd_o) + jax.nn.sigmoid(g_a) * (y_a @ w_att_o)
    y = x + merged @ w_out
    return y, k, v, ki, conv_new, ssm_new


def setup_inputs(seed: int = 0) -> dict:
    key = jax.random.key(seed)
    ks = jax.random.split(key, 24)
    f32 = jnp.float32

    def nrm(k, shape, s):
        return jax.random.normal(k, shape, f32) * s

    dt0 = jnp.exp(jax.random.uniform(ks[11], (DEPTH, N_SSD_HEADS), f32, math.log(1e-3), math.log(1e-1)))
    return {
        'x_prompt': nrm(ks[0], (BATCH, SEQ, D_MODEL), 1.0),
        'x_sample': nrm(ks[1], (DEC_BATCH, DEC_SEQ, D_MODEL), 1.0),
        'cache_k': nrm(ks[2], (DEPTH, DEC_BATCH, PAST_LEN, N_KV, HEAD_DIM), 1.0),
        'cache_v': nrm(ks[3], (DEPTH, DEC_BATCH, PAST_LEN, N_KV, HEAD_DIM), 1.0),
        'cache_idx_k': nrm(ks[4], (DEPTH, DEC_BATCH, PAST_LEN, D_IDX), 1.0),
        'state_conv': nrm(ks[5], (DEPTH, DEC_BATCH, CONV_W - 1, CONV_DIM), 1.0),
        'state_ssm': nrm(ks[6], (DEPTH, DEC_BATCH, N_SSD_HEADS, SSD_HEAD_DIM, D_STATE), 0.1),
        'norm_g': 1.0 + nrm(ks[7], (DEPTH, D_MODEL), 0.02),
        'w_in': nrm(ks[8], (DEPTH, D_MODEL, D_PROJ), D_MODEL ** -0.5),
        'conv_w': nrm(ks[9], (DEPTH, CONV_W, CONV_DIM), CONV_W ** -0.5),
        'conv_b': nrm(ks[10], (DEPTH, CONV_DIM), 0.02),
        'dt_bias': dt0 + jnp.log(-jnp.expm1(-dt0)),
        'a_log': jnp.log(jax.random.uniform(ks[12], (DEPTH, N_SSD_HEADS), f32, 1.0, 16.0)),
        'd_skip': 1.0 + nrm(ks[13], (DEPTH, N_SSD_HEADS), 0.02),
        'ssd_norm_g': 1.0 + nrm(ks[14], (DEPTH, D_INNER), 0.02),
        'q_norm_g': 1.0 + nrm(ks[15], (DEPTH, HEAD_DIM), 0.02),
        'k_norm_g': 1.0 + nrm(ks[16], (DEPTH, HEAD_DIM), 0.02),
        'idx_k_norm_g': 1.0 + nrm(ks[17], (DEPTH, D_IDX), 0.02),
        'w_ssd_o': nrm(ks[18], (DEPTH, D_INNER, D_MODEL), D_INNER ** -0.5),
        'w_att_o': nrm(ks[19], (DEPTH, D_ATT, D_MODEL), D_ATT ** -0.5),
        'w_out': nrm(ks[20], (DEPTH, D_MODEL, D_MODEL), D_MODEL ** -0.5),
    }


def reference(x_prompt, x_sample, cache_k, cache_v, cache_idx_k, state_conv, state_ssm, norm_g, w_in,
              conv_w, conv_b, dt_bias, a_log, d_skip, ssd_norm_g, q_norm_g, k_norm_g, idx_k_norm_g,
              w_ssd_o, w_att_o, w_out):
    yp, ys = x_prompt, x_sample
    b_p = x_prompt.shape[0]
    kp, vp, kip, cp, sp = [], [], [], [], []
    kd, vd, kid, cd, sd = [], [], [], [], []
    for l in range(DEPTH):
        wts = (norm_g[l], w_in[l], conv_w[l], conv_b[l], dt_bias[l], a_log[l], d_skip[l], ssd_norm_g[l],
               q_norm_g[l], k_norm_g[l], idx_k_norm_g[l], w_ssd_o[l], w_att_o[l], w_out[l])
        conv0 = jnp.zeros((b_p, CONV_W - 1, CONV_DIM), x_prompt.dtype)
        ssm0 = jnp.zeros((b_p, N_SSD_HEADS, SSD_HEAD_DIM, D_STATE), jnp.float32)
        yp, k1, v1, ki1, c1, s1 = mixer_layer(yp, conv0, ssm0, None, None, None, *wts)
        ys, k2, v2, ki2, c2, s2 = mixer_layer(ys, state_conv[l], state_ssm[l], cache_k[l], cache_v[l],
                                              cache_idx_k[l], *wts)
        kp.append(k1); vp.append(v1); kip.append(ki1); cp.append(c1); sp.append(s1)
        kd.append(k2); vd.append(v2); kid.append(ki2); cd.append(c2); sd.append(s2)
    k_prompt, v_prompt, idxk_prompt = jnp.stack(kp), jnp.stack(vp), jnp.stack(kip)
    conv_prompt, ssm_prompt = jnp.stack(cp), jnp.stack(sp)
    k_sample, v_sample, idxk_sample = jnp.stack(kd), jnp.stack(vd), jnp.stack(kid)
    conv_sample, ssm_sample = jnp.stack(cd), jnp.stack(sd)
    return (yp, ys, k_prompt, v_prompt, idxk_prompt, conv_prompt, ssm_prompt,
            k_sample, v_sample, idxk_sample, conv_sample, ssm_sample)
```

```python
import functools
import math

import jax
import jax.numpy as jnp
from jax import lax
from jax.experimental import pallas as pl
from jax.experimental.pallas import tpu as pltpu

D_MODEL = 1024
CHUNK = 64
D_INNER = 2048
SSD_HEAD_DIM = 64
N_SSD_HEADS = 32
N_GROUPS = 4
HEADS_PER_GROUP = N_SSD_HEADS // N_GROUPS
D_STATE = 128
CONV_W = 4
CONV_DIM = D_INNER + 2 * N_GROUPS * D_STATE
HEAD_DIM = 64
N_HEADS = 16
N_KV = 4
KV_GROUP = N_HEADS // N_KV
D_ATT = N_HEADS * HEAD_DIM
N_IDX_HEADS = 8
D_IDX = 64
TOPK_MAX = 256
EPS = 1e-6

OFF_XBC = 0
OFF_GS = 3072
OFF_ZS = 4096
OFF_Q = 6144
OFF_ZA = 7168
OFF_GA = 8192
OFF_QI = 9216
OFF_K = 9728
OFF_V = 9984
OFF_MISC = 10240
N_PROJ = 10368
MISC_DT = 0
MISC_WI = 32
MISC_KI = 64

LANES = 128
VMEM_LIMIT = 56 * 1024 * 1024
INT_MIN = -(2 ** 31)
NEG_BIG = -1e30

F32 = jnp.float32
BF16 = jnp.bfloat16


def _cparams(sem):
    return pltpu.CompilerParams(dimension_semantics=sem, vmem_limit_bytes=VMEM_LIMIT)


def _sigmoid(x):
    return 1.0 / (1.0 + jnp.exp(-x))


def _silu(x):
    return x * _sigmoid(x)


def _softplus(x):
    return jnp.maximum(x, 0.0) + jnp.log(1.0 + jnp.exp(-jnp.abs(x)))


def _split2(v):
    hi = v.astype(BF16)
    lo = (v - hi.astype(F32)).astype(BF16)
    return hi, lo


def _split3(v):
    hi = v.astype(BF16)
    r = v - hi.astype(F32)
    mid = r.astype(BF16)
    lo = (r - mid.astype(F32)).astype(BF16)
    return hi, mid, lo


def _dot(a, b):
    return jnp.dot(a, b, preferred_element_type=F32)


def _dot_nt(a, b):
    return lax.dot_general(a, b, (((1,), (1,)), ((), ())), preferred_element_type=F32)


def _proj_kernel(x_ref, g_ref, w_ref, o_ref):
    x = x_ref[...]
    ms = jnp.mean(x * x, axis=-1, keepdims=True)
    h = (x * lax.rsqrt(ms + EPS) * g_ref[...]).astype(BF16)
    o_ref[...] = _dot(h, w_ref[...])


def _proj(x2d, norm_g, w_perm):
    t = x2d.shape[0]
    tm = min(512, t)
    tn = 3456
    return pl.pallas_call(
        _proj_kernel,
        grid=(N_PROJ // tn, t // tm),
        in_specs=[
            pl.BlockSpec((tm, D_MODEL), lambda j, i: (i, 0)),
            pl.BlockSpec((1, D_MODEL), lambda j, i: (0, 0)),
            pl.BlockSpec((D_MODEL, tn), lambda j, i: (0, j)),
        ],
        out_specs=pl.BlockSpec((tm, tn), lambda j, i: (i, j)),
        out_shape=jax.ShapeDtypeStruct((t, N_PROJ), F32),
        compiler_params=_cparams(("arbitrary", "arbitrary")),
        name="proj",
    )(x2d, norm_g, w_perm)


def _ssd_kernel(xbc_ref, zs_ref, gs_ref, misc_ref, hist_ref, h0_ref, convw_ref, convb_ref, dtb_ref,
                a_ref, dskip_ref, ng_ref, wo_ref,
                os_ref, convn_ref, ssm_ref,
                pad_ref, ht_ref, xs_ref, b_ref, c_ref, y_ref, *, tc):
    c = pl.program_id(1)
    nc = pl.num_programs(1)
    hp = SSD_HEAD_DIM
    gw = HEADS_PER_GROUP * hp

    @pl.when(c == 0)
    def _():
        pad_ref[5:8, :] = hist_ref[0]
        for g in range(N_GROUPS):
            hg = h0_ref[0, g * HEADS_PER_GROUP:(g + 1) * HEADS_PER_GROUP].reshape(gw, D_STATE)
            ht_ref[g] = hg.T

    pad_ref[8:8 + tc, :] = xbc_ref[...]
    cw = 512
    for ct in range(CONV_DIM // cw):
        cs = slice(ct * cw, (ct + 1) * cw)
        acc = convb_ref[:, cs] + pad_ref[5:5 + tc, cs] * convw_ref[0:1, cs]
        acc = acc + pad_ref[6:6 + tc, cs] * convw_ref[1:2, cs]
        acc = acc + pad_ref[7:7 + tc, cs] * convw_ref[2:3, cs]
        acc = acc + pad_ref[8:8 + tc, cs] * convw_ref[3:4, cs]
        u = _silu(acc)
        if ct < D_INNER // cw:
            xs_ref[:, cs] = u
        elif ct == D_INNER // cw:
            b_ref[...] = u
        else:
            c_ref[...] = u.astype(BF16)
    last3 = pad_ref[8 + tc - 3:8 + tc, :]
    pad_ref[5:8, :] = last3

    @pl.when(c == nc - 1)
    def _():
        convn_ref[0] = last3

    dt = _softplus(misc_ref[...] + dtb_ref[...])
    la = dt * a_ref[...]
    row = lax.broadcasted_iota(jnp.int32, (tc, tc), 0)
    col = lax.broadcasted_iota(jnp.int32, (tc, tc), 1)
    tril = col <= row
    trilb = jnp.where(tril, 1.0, 0.0).astype(BF16)
    acum = None
    for part in _split3(la):
        d = _dot(trilb, part)
        acum = d if acum is None else acum + d
    acum_t = acum.T
    a_last = acum[tc - 1:tc, :]
    ea = jnp.exp(acum)
    dend = jnp.exp(a_last - acum)
    atot = jnp.exp(a_last)

    for g in range(N_GROUPS):
        bg = b_ref[:, g * D_STATE:(g + 1) * D_STATE]
        cg = c_ref[:, g * D_STATE:(g + 1) * D_STATE]
        bgb = bg.astype(BF16)
        cb = _dot_nt(cg, bgb)
        h_old = ht_ref[g]
        yoff = _dot(cg, h_old.astype(BF16))
        dxs = []
        decs = []
        for e in range(HEADS_PER_GROUP):
            h = g * HEADS_PER_GROUP + e
            hs = slice(h * hp, (h + 1) * hp)
            xs_h = xs_ref[:, hs]
            xk = xs_h * dt[:, h:h + 1]
            seg = acum[:, h:h + 1] - acum_t[h:h + 1, :]
            lmat = jnp.exp(jnp.where(tril, seg, -jnp.inf))
            w = (cb * lmat).astype(BF16)
            yd = _dot(w, xk.astype(BF16))
            y_h = yd + yoff[:, e * hp:(e + 1) * hp] * ea[:, h:h + 1] + dskip_ref[:, h:h + 1] * xs_h
            y_ref[:, hs] = y_h
            dxs.append((xk * dend[:, h:h + 1]).astype(BF16))
            decs.append(jnp.broadcast_to(atot[:, h:h + 1], (1, hp)))
        dx = jnp.concatenate(dxs, axis=1)
        dec = jnp.concatenate(decs, axis=1)
        st = _dot(bg.T.astype(BF16), dx)
        ht_ref[g] = dec * h_old + st

    out = None
    for g in range(N_GROUPS):
        gs_ = slice(g * gw, (g + 1) * gw)
        yg = y_ref[:, gs_] * _silu(zs_ref[:, gs_])
        ms = jnp.mean(yg * yg, axis=-1, keepdims=True)
        yn = (yg * lax.rsqrt(ms + EPS) * ng_ref[:, gs_]).astype(BF16)
        d = _dot(yn, wo_ref[gs_, :])
        out = d if out is None else out + d
    os_ref[...] = _sigmoid(gs_ref[...]) * out

    @pl.when(c == nc - 1)
    def _():
        for g in range(N_GROUPS):
            ssm_ref[0, g * HEADS_PER_GROUP:(g + 1) * HEADS_PER_GROUP] = (
                ht_ref[g].T.reshape(HEADS_PER_GROUP, hp, D_STATE))


def _ssd(p, b, l, hist, h0, conv_w, conv_b, dtb, a_neg, dskip, ng, wo_b):
    tc = min(128, l)
    nc = l // tc
    row = lambda bi, ci: bi * nc + ci
    kern = functools.partial(_ssd_kernel, tc=tc)
    full2 = lambda bi, ci: (0, 0)
    return pl.pallas_call(
        kern,
        grid=(b, nc),
        in_specs=[
            pl.BlockSpec((tc, CONV_DIM), lambda bi, ci: (row(bi, ci), OFF_XBC // CONV_DIM)),
            pl.BlockSpec((tc, D_INNER), lambda bi, ci: (row(bi, ci), OFF_ZS // D_INNER)),
            pl.BlockSpec((tc, D_MODEL), lambda bi, ci: (row(bi, ci), OFF_GS // D_MODEL)),
            pl.BlockSpec((tc, LANES), lambda bi, ci: (row(bi, ci), OFF_MISC // LANES)),
            pl.BlockSpec((1, CONV_W - 1, CONV_DIM), lambda bi, ci: (bi, 0, 0)),
            pl.BlockSpec((1, N_SSD_HEADS, SSD_HEAD_DIM, D_STATE), lambda bi, ci: (bi, 0, 0, 0)),
            pl.BlockSpec((CONV_W, CONV_DIM), full2),
            pl.BlockSpec((1, CONV_DIM), full2),
            pl.BlockSpec((1, LANES), full2),
            pl.BlockSpec((1, LANES), full2),
            pl.BlockSpec((1, LANES), full2),
            pl.BlockSpec((1, D_INNER), full2),
            pl.BlockSpec((D_INNER, D_MODEL), full2),
        ],
        out_specs=[
            pl.BlockSpec((tc, D_MODEL), lambda bi, ci: (row(bi, ci), 0)),
            pl.BlockSpec((1, CONV_W - 1, CONV_DIM), lambda bi, ci: (bi, 0, 0)),
            pl.BlockSpec((1, N_SSD_HEADS, SSD_HEAD_DIM, D_STATE), lambda bi, ci: (bi, 0, 0, 0)),
        ],
        out_shape=[
            jax.ShapeDtypeStruct((b * l, D_MODEL), F32),
            jax.ShapeDtypeStruct((b, CONV_W - 1, CONV_DIM), F32),
            jax.ShapeDtypeStruct((b, N_SSD_HEADS, SSD_HEAD_DIM, D_STATE), F32),
        ],
        scratch_shapes=[
            pltpu.VMEM((8 + tc, CONV_DIM), F32),
            pltpu.VMEM((N_GROUPS, D_STATE, HEADS_PER_GROUP * SSD_HEAD_DIM), F32),
            pltpu.VMEM((tc, D_INNER), F32),
            pltpu.VMEM((tc, N_GROUPS * D_STATE), F32),
            pltpu.VMEM((tc, N_GROUPS * D_STATE), BF16),
            pltpu.VMEM((tc, D_INNER), F32),
        ],
        compiler_params=_cparams(("arbitrary", "arbitrary")),
        name="ssd",
    )(p, p, p, p, hist, h0, conv_w, conv_b, dtb, a_neg, dskip, ng, wo_b)


def _seg_mean64(xsq, bd):
    hi, lo = _split2(xsq)
    return (_dot(hi, bd) + _dot(lo, bd)) * (1.0 / 64.0)


def _prep_kernel(q_ref, k_ref, v_ref, qi_ref, misc_ref, qg_ref, kg_ref, ig_ref, bd_ref,
                 qn_ref, kn_ref, vv_ref, kin_ref, kb_ref, vb_ref, kib_ref, qib_ref, wi_ref):
    bd = bd_ref[...]
    w = 256
    for ct in range(D_ATT // w):
        cs = slice(ct * w, (ct + 1) * w)
        q = q_ref[:, cs]
        qn = q * lax.rsqrt(_seg_mean64(q * q, bd) + EPS) * qg_ref[:, cs]
        qn_ref[0, :, cs] = (qn * (HEAD_DIM ** -0.5)).astype(BF16)
    k = k_ref[...]
    kn = k * lax.rsqrt(_seg_mean64(k * k, bd) + EPS) * kg_ref[...]
    kn_ref[0] = kn
    v = v_ref[...]
    vv_ref[0] = v
    knb = kn.astype(BF16)
    vb = v.astype(BF16)
    for g in range(N_KV):
        kb_ref[0, g] = knb[:, g * HEAD_DIM:(g + 1) * HEAD_DIM]
        vb_ref[0, g] = vb[:, g * HEAD_DIM:(g + 1) * HEAD_DIM]
    misc = misc_ref[...]
    mn = misc * lax.rsqrt(_seg_mean64(misc * misc, bd[:LANES, :LANES]) + EPS) * ig_ref[...]
    kin = mn[:, MISC_KI:MISC_KI + D_IDX]
    kin_ref[0] = kin
    kib_ref[0] = kin.astype(BF16)
    qib_ref[0] = (qi_ref[...] * (D_IDX ** -0.5)).astype(BF16)
    wi_ref[0] = misc[:, MISC_WI:MISC_WI + N_IDX_HEADS] * (N_IDX_HEADS ** -0.5)


def _prep(p, b, l, qg, kg, ig, bd):
    tm = min(256, l)
    nt = l // tm
    row = lambda bi, ti: bi * nt + ti
    kvw = N_KV * HEAD_DIM
    full2 = lambda bi, ti: (0, 0)
    tok3 = lambda bi, ti: (bi, ti, 0)
    return pl.pallas_call(
        _prep_kernel,
        grid=(b, nt),
        in_specs=[
            pl.BlockSpec((tm, D_ATT), lambda bi, ti: (row(bi, ti), OFF_Q // D_ATT)),
            pl.BlockSpec((tm, kvw), lambda bi, ti: (row(bi, ti), OFF_K // kvw)),
            pl.BlockSpec((tm, kvw), lambda bi, ti: (row(bi, ti), OFF_V // kvw)),
            pl.BlockSpec((tm, N_IDX_HEADS * D_IDX), lambda bi, ti: (row(bi, ti), OFF_QI // (N_IDX_HEADS * D_IDX))),
            pl.BlockSpec((tm, LANES), lambda bi, ti: (row(bi, ti), OFF_MISC // LANES)),
            pl.BlockSpec((1, D_ATT), full2),
            pl.BlockSpec((1, kvw), full2),
            pl.BlockSpec((1, LANES), full2),
            pl.BlockSpec((256, 256), full2),
        ],
        out_specs=[
            pl.BlockSpec((1, tm, D_ATT), tok3),
            pl.BlockSpec((1, tm, kvw), tok3),
            pl.BlockSpec((1, tm, kvw), tok3),
            pl.BlockSpec((1, tm, D_IDX), tok3),
            pl.BlockSpec((1, N_KV, tm, HEAD_DIM), lambda bi, ti: (bi, 0, ti, 0)),
            pl.BlockSpec((1, N_KV, tm, HEAD_DIM), lambda bi, ti: (bi, 0, ti, 0)),
            pl.BlockSpec((1, tm, D_IDX), tok3),
            pl.BlockSpec((1, tm, N_IDX_HEADS * D_IDX), tok3),
            pl.BlockSpec((1, tm, N_IDX_HEADS), tok3),
        ],
        out_shape=[
            jax.ShapeDtypeStruct((b, l, D_ATT), BF16),
            jax.ShapeDtypeStruct((b, l, kvw), F32),
            jax.ShapeDtypeStruct((b, l, kvw), F32),
            jax.ShapeDtypeStruct((b, l, D_IDX), F32),
            jax.ShapeDtypeStruct((b, N_KV, l, HEAD_DIM), BF16),
            jax.ShapeDtypeStruct((b, N_KV, l, HEAD_DIM), BF16),
            jax.ShapeDtypeStruct((b, l, D_IDX), BF16),
            jax.ShapeDtypeStruct((b, l, N_IDX_HEADS * D_IDX), BF16),
            jax.ShapeDtypeStruct((b, l, N_IDX_HEADS), F32),
        ],
        compiler_params=_cparams(("arbitrary", "arbitrary")),
        name="prep",
    )(p, p, p, p, p, qg, kg, ig, bd)


def _alibi_slope(h):
    return 2.0 ** (-8.0 * (h + 1) / N_HEADS)


def _dsa_kernel(qn_ref, qi_ref, wi_ref, k_ref, v_ref, ki_ref, o_ref,
                key_ref, qg_ref, qih_ref, wb_ref, m_ref, l_ref, acc_ref, sel_ref,
                *, tq, tk, past, n_sel, s_pad):
    i = pl.program_id(1)
    q0 = past + i * tq
    nkt = (q0 + tq + tk - 1) // tk
    nsub = tk // LANES
    qpos = q0 + lax.broadcasted_iota(jnp.int32, (tq, tk), 0)
    lane = lax.broadcasted_iota(jnp.int32, (tq, tk), 1)

    for h in range(N_IDX_HEADS):
        qih_ref[h] = qi_ref[0, :, h * D_IDX:(h + 1) * D_IDX]
        wb_ref[h] = jnp.broadcast_to(wi_ref[0, :, h:h + 1], (tq, LANES))
    for g in range(N_KV):
        for e in range(KV_GROUP):
            h = g * KV_GROUP + e
            qg_ref[g, e * tq:(e + 1) * tq, :] = qn_ref[0, :, h * HEAD_DIM:(h + 1) * HEAD_DIM]

    def score_body(kt, carry):
        ks = pl.multiple_of(kt * tk, tk)
        ki_t = ki_ref[0, pl.ds(ks, tk), :]
        acc = jnp.zeros((tq, tk), F32)
        for h in range(N_IDX_HEADS):
            s = _dot_nt(qih_ref[h], ki_t)
            wfull = jnp.concatenate([wb_ref[h]] * nsub, axis=1)
            acc = acc + wfull * jnp.maximum(s, 0.0)
        bits = lax.bitcast_convert_type(acc, jnp.int32)
        key = bits ^ ((bits >> 31) & 0x7FFFFFFF)
        adm = ((ks + lane) >> 6) <= (qpos >> 6)
        key_ref[:, pl.ds(ks, tk)] = jnp.where(adm, key, INT_MIN)
        return carry

    lax.fori_loop(0, nkt, score_body, 0)

    def count(pred):
        def body(kt, part):
            ks = pl.multiple_of(kt * tk, tk)
            hit = jnp.where(pred(key_ref[:, pl.ds(ks, tk)], ks + lane), 1.0, 0.0)
            for j in range(nsub):
                part = part + hit[:, j * LANES:(j + 1) * LANES]
            return part
        part = lax.fori_loop(0, nkt, body, jnp.zeros((tq, LANES), F32))
        return jnp.sum(part, axis=1, keepdims=True)

    def bit_body(bidx, t):
        cand = t | lax.shift_left(jnp.int32(1), 31 - bidx)
        cand_s = cand ^ INT_MIN
        cnt = count(lambda key, kidx: key >= cand_s)
        return jnp.where(cnt >= n_sel, cand, t)

    t_u = lax.fori_loop(0, 32, bit_body, jnp.zeros((tq, 1), jnp.int32))
    thr = t_u ^ INT_MIN

    cnt_gt = count(lambda key, kidx: key > thr)
    cnt_ge = count(lambda key, kidx: key >= thr)
    need = n_sel - cnt_gt
    short = t_u == 0
    sel_ref[:, 0:1] = jnp.where(short, 0, s_pad).astype(jnp.int32)
    excess = jnp.where((cnt_ge > n_sel) & jnp.logical_not(short), 1.0, 0.0)

    @pl.when(jnp.max(excess) > 0.0)
    def _():
        nbits = max(1, (s_pad - 1).bit_length())

        def idx_body(bidx, m):
            cand = m | lax.shift_left(jnp.int32(1), nbits - 1 - bidx)
            cnt = count(lambda key, kidx: (key == thr) & (kidx < cand))
            return jnp.where(cnt < need, cand, m)

        m = lax.fori_loop(0, nbits, idx_body, jnp.zeros((tq, 1), jnp.int32))
        sel_ref[:, 0:1] = jnp.where(short, 0, m + 1)

    jlim = sel_ref[:, 0:1]

    m_ref[...] = jnp.full(m_ref.shape, NEG_BIG, F32)
    l_ref[...] = jnp.zeros(l_ref.shape, F32)
    acc_ref[...] = jnp.zeros(acc_ref.shape, F32)

    def att_body(kt, carry):
        ks = pl.multiple_of(kt * tk, tk)
        key = key_ref[:, pl.ds(ks, tk)]
        kidx = ks + lane
        sel = (key > thr) | ((key == thr) & (kidx < jlim))
        mbias = jnp.where(sel, 0.0, -jnp.inf)
        dist = jnp.abs(qpos - kidx).astype(F32)
        for g in range(N_KV):
            k_t = k_ref[0, g, pl.ds(ks, tk), :]
            v_t = v_ref[0, g, pl.ds(ks, tk), :]
            s = _dot_nt(qg_ref[g], k_t)
            for e in range(KV_GROUP):
                h = g * KV_GROUP + e
                r = slice(h * tq, (h + 1) * tq)
                lg = s[e * tq:(e + 1) * tq] - _alibi_slope(h) * dist + mbias
                m_old = m_ref[r, :]
                m_new = jnp.maximum(m_old, jnp.max(lg, axis=1, keepdims=True))
                alpha = jnp.exp(m_old - m_new)
                p = jnp.exp(lg - m_new[:, 0:1])
                l_ref[r, :] = alpha * l_ref[r, :] + jnp.sum(p, axis=1, keepdims=True)
                m_ref[r, :] = m_new
                acc_ref[r, :] = alpha[:, 0:HEAD_DIM] * acc_ref[r, :] + _dot(p.astype(BF16), v_t)
        return carry

    lax.fori_loop(0, nkt, att_body, 0)

    for h in range(N_HEADS):
        r = slice(h * tq, (h + 1) * tq)
        o_ref[0, :, h * HEAD_DIM:(h + 1) * HEAD_DIM] = acc_ref[r, :] / l_ref[r, 0:HEAD_DIM]


def _dsa(qn, qib, wis, k_all, v_all, ki_all, past, n_sel):
    b, l, _ = qn.shape
    s_pad = k_all.shape[2]
    tq = min(128, l)
    tk = 512
    assert s_pad % tk == 0 and l % tq == 0 and past % CHUNK == 0 and tq % CHUNK == 0
    kern = functools.partial(_dsa_kernel, tq=tq, tk=tk, past=past, n_sel=n_sel, s_pad=s_pad)
    tok3 = lambda bi, qi: (bi, qi, 0)
    return pl.pallas_call(
        kern,
        grid=(b, l // tq),
        in_specs=[
            pl.BlockSpec((1, tq, D_ATT), tok3),
            pl.BlockSpec((1, tq, N_IDX_HEADS * D_IDX), tok3),
            pl.BlockSpec((1, tq, N_IDX_HEADS), tok3),
            pl.BlockSpec((1, N_KV, s_pad, HEAD_DIM), lambda bi, qi: (bi, 0, 0, 0)),
            pl.BlockSpec((1, N_KV, s_pad, HEAD_DIM), lambda bi, qi: (bi, 0, 0, 0)),
            pl.BlockSpec((1, s_pad, D_IDX), lambda bi, qi: (bi, 0, 0)),
        ],
        out_specs=pl.BlockSpec((1, tq, D_ATT), tok3),
        out_shape=jax.ShapeDtypeStruct((b, l, D_ATT), F32),
        scratch_shapes=[
            pltpu.VMEM((tq, s_pad), jnp.int32),
            pltpu.VMEM((N_KV, KV_GROUP * tq, HEAD_DIM), BF16),
            pltpu.VMEM((N_IDX_HEADS, tq, D_IDX), BF16),
            pltpu.VMEM((N_IDX_HEADS, tq, LANES), F32),
            pltpu.VMEM((N_HEADS * tq, LANES), F32),
            pltpu.VMEM((N_HEADS * tq, LANES), F32),
            pltpu.VMEM((N_HEADS * tq, HEAD_DIM), F32),
            pltpu.VMEM((tq, LANES), jnp.int32),
        ],
        compiler_params=_cparams(("arbitrary", "arbitrary")),
        name="dsa",
    )(qn, qib, wis, k_all, v_all, ki_all)


def _merge_kernel(x_ref, oa_ref, za_ref, ga_ref, os_ref, wa_ref, wout_ref, y_ref):
    ya = (oa_ref[...] * _silu(za_ref[...])).astype(BF16)
    merged = os_ref[...] + _sigmoid(ga_ref[...]) * _dot(ya, wa_ref[...])
    y_ref[...] = x_ref[...] + _dot(merged.astype(BF16), wout_ref[...])


def _merge(x2d, oa2d, p, os2d, wa_b, wout_b):
    t = x2d.shape[0]
    tm = min(512, t)
    tok = lambda i: (i, 0)
    full = lambda i: (0, 0)
    return pl.pallas_call(
        _merge_kernel,
        grid=(t // tm,),
        in_specs=[
            pl.BlockSpec((tm, D_MODEL), tok),
            pl.BlockSpec((tm, D_ATT), tok),
            pl.BlockSpec((tm, D_ATT), lambda i: (i, OFF_ZA // D_ATT)),
            pl.BlockSpec((tm, D_MODEL), lambda i: (i, OFF_GA // D_MODEL)),
            pl.BlockSpec((tm, D_MODEL), tok),
            pl.BlockSpec((D_ATT, D_MODEL), full),
            pl.BlockSpec((D_MODEL, D_MODEL), full),
        ],
        out_specs=pl.BlockSpec((tm, D_MODEL), tok),
        out_shape=jax.ShapeDtypeStruct((t, D_MODEL), F32),
        compiler_params=_cparams(("arbitrary",)),
        name="merge",
    )(x2d, oa2d, p, p, os2d, wa_b, wout_b)


def _pad_lanes(v, n=LANES):
    v = v.reshape(1, -1).astype(F32)
    return jnp.pad(v, ((0, 0), (0, n - v.shape[1])))


def _layer_weights(norm_g, w_in, conv_w, conv_b, dt_bias, a_log, d_skip, ssd_norm_g, q_norm_g, k_norm_g,
                   idx_k_norm_g, w_ssd_o, w_att_o, w_out):
    sizes = (D_INNER, CONV_DIM, N_SSD_HEADS, D_ATT, N_KV * HEAD_DIM, N_KV * HEAD_DIM, D_ATT,
             N_IDX_HEADS * D_IDX, D_IDX, N_IDX_HEADS, D_MODEL, D_MODEL)
    parts, off = [], 0
    for n in sizes:
        parts.append(w_in[:, off:off + n])
        off += n
    z_s, xbc, dt, q, k, v, z_a, qi, ki, wi, g_s, g_a = parts
    pad = jnp.zeros((D_MODEL, LANES - N_SSD_HEADS - N_IDX_HEADS - D_IDX), w_in.dtype)
    w_perm = jnp.concatenate([xbc, g_s, z_s, q, z_a, g_a, qi, k, v, dt, wi, pad, ki], axis=1).astype(BF16)
    seg = jnp.arange(256) // 64
    return dict(
        norm_g=norm_g.reshape(1, D_MODEL),
        w_perm=w_perm,
        conv_w=conv_w,
        conv_b=conv_b.reshape(1, CONV_DIM),
        dtb=_pad_lanes(dt_bias),
        a_neg=_pad_lanes(-jnp.exp(a_log.astype(F32))),
        dskip=_pad_lanes(d_skip),
        ng=ssd_norm_g.reshape(1, D_INNER),
        wo_b=w_ssd_o.astype(BF16),
        qg=jnp.tile(q_norm_g.reshape(1, HEAD_DIM), (1, N_HEADS)),
        kg=jnp.tile(k_norm_g.reshape(1, HEAD_DIM), (1, N_KV)),
        ig=jnp.concatenate([jnp.ones((1, LANES - D_IDX), F32), idx_k_norm_g.reshape(1, D_IDX)], axis=1),
        bd=(seg[:, None] == seg[None, :]).astype(BF16),
        wa_b=w_att_o.astype(BF16),
        wout_b=w_out.astype(BF16),
    )


def _head_major_bf16(t):
    return jnp.transpose(t, (0, 2, 1, 3)).astype(BF16)


def _mixer_layer(x, hist, h0, k_past, v_past, ki_past, w):
    b, l, _ = x.shape
    x2d = x.reshape(b * l, D_MODEL)
    p = _proj(x2d, w["norm_g"], w["w_perm"])
    os2d, conv_new, ssm_new = _ssd(p, b, l, hist, h0, w["conv_w"], w["conv_b"], w["dtb"], w["a_neg"],
                                   w["dskip"], w["ng"], w["wo_b"])
    qn, kn, vv, kin, kb, vb, kib, qib, wis = _prep(p, b, l, w["qg"], w["kg"], w["ig"], w["bd"])
    past = 0 if k_past is None else k_past.shape[1]
    s = past + l
    s_pad = -(-s // 512) * 512
    if k_past is not None:
        kb = jnp.concatenate([_head_major_bf16(k_past), kb], axis=2)
        vb = jnp.concatenate([_head_major_bf16(v_past), vb], axis=2)
        kib = jnp.concatenate([ki_past.astype(BF16), kib], axis=1)
    if s_pad != s:
        kb = jnp.pad(kb, ((0, 0), (0, 0), (0, s_pad - s), (0, 0)))
        vb = jnp.pad(vb, ((0, 0), (0, 0), (0, s_pad - s), (0, 0)))
        kib = jnp.pad(kib, ((0, 0), (0, s_pad - s), (0, 0)))
    n_sel = min(TOPK_MAX, s // 4)
    oa = _dsa(qn, qib, wis, kb, vb, kib, past, n_sel)
    y2d = _merge(x2d, oa.reshape(b * l, D_ATT), p, os2d, w["wa_b"], w["wout_b"])
    return (y2d.reshape(b, l, D_MODEL), kn.reshape(b, l, N_KV, HEAD_DIM), vv.reshape(b, l, N_KV, HEAD_DIM),
            kin, conv_new, ssm_new)


def kernel(x_prompt, x_sample, cache_k, cache_v, cache_idx_k, state_conv, state_ssm, norm_g, w_in, conv_w,
           conv_b, dt_bias, a_log, d_skip, ssd_norm_g, q_norm_g, k_norm_g, idx_k_norm_g, w_ssd_o, w_att_o,
           w_out):
    depth = w_in.shape[0]
    yp, ys = x_prompt, x_sample
    b_p = x_prompt.shape[0]
    outs_p, outs_s = [], []
    for lyr in range(depth):
        w = _layer_weights(norm_g[lyr], w_in[lyr], conv_w[lyr], conv_b[lyr], dt_bias[lyr], a_log[lyr],
                           d_skip[lyr], ssd_norm_g[lyr], q_norm_g[lyr], k_norm_g[lyr], idx_k_norm_g[lyr],
                           w_ssd_o[lyr], w_att_o[lyr], w_out[lyr])
        conv0 = jnp.zeros((b_p, CONV_W - 1, CONV_DIM), F32)
        ssm0 = jnp.zeros((b_p, N_SSD_HEADS, SSD_HEAD_DIM, D_STATE), F32)
        yp, *rest_p = _mixer_layer(yp, conv0, ssm0, None, None, None, w)
        ys, *rest_s = _mixer_layer(ys, state_conv[lyr], state_ssm[lyr], cache_k[lyr], cache_v[lyr],
                                   cache_idx_k[lyr], w)
        outs_p.append(rest_p)
        outs_s.append(rest_s)
    stack = lambda outs, j: jnp.stack([o[j] for o in outs])
    return (yp, ys,
            stack(outs_p, 0), stack(outs_p, 1), stack(outs_p, 2), stack(outs_p, 3), stack(outs_p, 4),
            stack(outs_s, 0), stack(outs_s, 1), stack(outs_s, 2), stack(outs_s, 3), stack(outs_s, 4))
```

```python
import functools
import math

import jax
import jax.numpy as jnp
from jax import lax
from jax.experimental import pallas as pl
from jax.experimental.pallas import tpu as pltpu

D_MODEL = 1024
CHUNK = 64
D_INNER = 2048
SSD_HEAD_DIM = 64
N_SSD_HEADS = 32
N_GROUPS = 4
HEADS_PER_GROUP = N_SSD_HEADS // N_GROUPS
D_STATE = 128
CONV_W = 4
CONV_DIM = D_INNER + 2 * N_GROUPS * D_STATE
HEAD_DIM = 64
N_HEADS = 16
N_KV = 4
KV_GROUP = N_HEADS // N_KV
D_ATT = N_HEADS * HEAD_DIM
N_IDX_HEADS = 8
D_IDX = 64
TOPK_MAX = 256
EPS = 1e-6

OFF_XBC = 0
OFF_GS = 3072
OFF_ZS = 4096
OFF_Q = 6144
OFF_ZA = 7168
OFF_GA = 8192
OFF_QI = 9216
OFF_K = 9728
OFF_V = 9984
OFF_MISC = 10240
N_PROJ = 10368
MISC_DT = 0
MISC_WI = 32
MISC_KI = 64

LANES = 128
VMEM_LIMIT = 56 * 1024 * 1024
INT_MIN = -(2 ** 31)
NEG_BIG = -1e30
LOG2E = 1.4426950408889634

F32 = jnp.float32
BF16 = jnp.bfloat16


def _cparams(sem):
    return pltpu.CompilerParams(dimension_semantics=sem, vmem_limit_bytes=VMEM_LIMIT)


def _sigmoid(x):
    return 1.0 / (1.0 + jnp.exp(-x))


def _silu(x):
    return x * _sigmoid(x)


def _softplus(x):
    return jnp.maximum(x, 0.0) + jnp.log(1.0 + jnp.exp(-jnp.abs(x)))


def _split2(v):
    hi = v.astype(BF16)
    lo = (v - hi.astype(F32)).astype(BF16)
    return hi, lo


def _split3(v):
    hi = v.astype(BF16)
    r = v - hi.astype(F32)
    mid = r.astype(BF16)
    lo = (r - mid.astype(F32)).astype(BF16)
    return hi, mid, lo


def _dot(a, b):
    return jnp.dot(a, b, preferred_element_type=F32)


def _dot_nt(a, b):
    return lax.dot_general(a, b, (((1,), (1,)), ((), ())), preferred_element_type=F32)


def _proj_kernel(x_ref, g_ref, w_ref, o_ref):
    x = x_ref[...]
    ms = jnp.mean(x * x, axis=-1, keepdims=True)
    h = (x * lax.rsqrt(ms + EPS) * g_ref[...]).astype(BF16)
    o_ref[...] = _dot(h, w_ref[...])


def _proj(x2d, norm_g, w_perm):
    t = x2d.shape[0]
    tm = min(512, t)
    tn = 3456
    return pl.pallas_call(
        _proj_kernel,
        grid=(N_PROJ // tn, t // tm),
        in_specs=[
            pl.BlockSpec((tm, D_MODEL), lambda j, i: (i, 0)),
            pl.BlockSpec((1, D_MODEL), lambda j, i: (0, 0)),
            pl.BlockSpec((D_MODEL, tn), lambda j, i: (0, j)),
        ],
        out_specs=pl.BlockSpec((tm, tn), lambda j, i: (i, j)),
        out_shape=jax.ShapeDtypeStruct((t, N_PROJ), F32),
        compiler_params=_cparams(("arbitrary", "arbitrary")),
        name="proj",
    )(x2d, norm_g, w_perm)


def _ssd_kernel(xbc_ref, zs_ref, gs_ref, misc_ref, hist_ref, h0_ref, convw_ref, convb_ref, dtb_ref,
                a_ref, dskip_ref, ng_ref, wo_ref,
                os_ref, convn_ref, ssm_ref,
                pad_ref, ht_ref, xs_ref, b_ref, c_ref, y_ref, *, tc):
    c = pl.program_id(1)
    nc = pl.num_programs(1)
    hp = SSD_HEAD_DIM
    gw = HEADS_PER_GROUP * hp

    @pl.when(c == 0)
    def _():
        pad_ref[5:8, :] = hist_ref[0]
        for g in range(N_GROUPS):
            hg = h0_ref[0, g * HEADS_PER_GROUP:(g + 1) * HEADS_PER_GROUP].reshape(gw, D_STATE)
            ht_ref[g] = hg.T

    pad_ref[8:8 + tc, :] = xbc_ref[...]
    cw = 512
    for ct in range(CONV_DIM // cw):
        cs = slice(ct * cw, (ct + 1) * cw)
        acc = convb_ref[:, cs] + pad_ref[5:5 + tc, cs] * convw_ref[0:1, cs]
        acc = acc + pad_ref[6:6 + tc, cs] * convw_ref[1:2, cs]
        acc = acc + pad_ref[7:7 + tc, cs] * convw_ref[2:3, cs]
        acc = acc + pad_ref[8:8 + tc, cs] * convw_ref[3:4, cs]
        u = _silu(acc)
        if ct < D_INNER // cw:
            xs_ref[:, cs] = u
        elif ct == D_INNER // cw:
            b_ref[...] = u
        else:
            c_ref[...] = u.astype(BF16)
    last3 = pad_ref[8 + tc - 3:8 + tc, :]
    pad_ref[5:8, :] = last3

    @pl.when(c == nc - 1)
    def _():
        convn_ref[0] = last3

    dt = _softplus(misc_ref[...] + dtb_ref[...])
    la = dt * a_ref[...]
    row = lax.broadcasted_iota(jnp.int32, (tc, tc), 0)
    col = lax.broadcasted_iota(jnp.int32, (tc, tc), 1)
    tril = col <= row
    trilb = jnp.where(tril, 1.0, 0.0).astype(BF16)
    acum = None
    for part in _split3(la):
        d = _dot(trilb, part)
        acum = d if acum is None else acum + d
    acum_t = acum.T
    a_last = acum[tc - 1:tc, :]
    ea = jnp.exp(acum)
    dend = jnp.exp(a_last - acum)
    atot = jnp.exp(a_last)

    for g in range(N_GROUPS):
        bg = b_ref[:, g * D_STATE:(g + 1) * D_STATE]
        cg = c_ref[:, g * D_STATE:(g + 1) * D_STATE]
        bgb = bg.astype(BF16)
        cb = _dot_nt(cg, bgb)
        h_old = ht_ref[g]
        yoff = _dot(cg, h_old.astype(BF16))
        dxs = []
        decs = []
        for e in range(HEADS_PER_GROUP):
            h = g * HEADS_PER_GROUP + e
            hs = slice(h * hp, (h + 1) * hp)
            xs_h = xs_ref[:, hs]
            xk = xs_h * dt[:, h:h + 1]
            seg = acum[:, h:h + 1] - acum_t[h:h + 1, :]
            lmat = jnp.exp(jnp.where(tril, seg, -jnp.inf))
            w = (cb * lmat).astype(BF16)
            yd = _dot(w, xk.astype(BF16))
            y_h = yd + yoff[:, e * hp:(e + 1) * hp] * ea[:, h:h + 1] + dskip_ref[:, h:h + 1] * xs_h
            y_ref[:, hs] = y_h
            dxs.append((xk * dend[:, h:h + 1]).astype(BF16))
            decs.append(jnp.broadcast_to(atot[:, h:h + 1], (1, hp)))
        dx = jnp.concatenate(dxs, axis=1)
        dec = jnp.concatenate(decs, axis=1)
        st = _dot(bg.T.astype(BF16), dx)
        ht_ref[g] = dec * h_old + st

    out = None
    for g in range(N_GROUPS):
        gs_ = slice(g * gw, (g + 1) * gw)
        yg = y_ref[:, gs_] * _silu(zs_ref[:, gs_])
        ms = jnp.mean(yg * yg, axis=-1, keepdims=True)
        yn = (yg * lax.rsqrt(ms + EPS) * ng_ref[:, gs_]).astype(BF16)
        d = _dot(yn, wo_ref[gs_, :])
        out = d if out is None else out + d
    os_ref[...] = _sigmoid(gs_ref[...]) * out

    @pl.when(c == nc - 1)
    def _():
        for g in range(N_GROUPS):
            ssm_ref[0, g * HEADS_PER_GROUP:(g + 1) * HEADS_PER_GROUP] = (
                ht_ref[g].T.reshape(HEADS_PER_GROUP, hp, D_STATE))


def _ssd(p, b, l, hist, h0, conv_w, conv_b, dtb, a_neg, dskip, ng, wo_b):
    tc = min(128, l)
    nc = l // tc
    row = lambda bi, ci: bi * nc + ci
    kern = functools.partial(_ssd_kernel, tc=tc)
    full2 = lambda bi, ci: (0, 0)
    return pl.pallas_call(
        kern,
        grid=(b, nc),
        in_specs=[
            pl.BlockSpec((tc, CONV_DIM), lambda bi, ci: (row(bi, ci), OFF_XBC // CONV_DIM)),
            pl.BlockSpec((tc, D_INNER), lambda bi, ci: (row(bi, ci), OFF_ZS // D_INNER)),
            pl.BlockSpec((tc, D_MODEL), lambda bi, ci: (row(bi, ci), OFF_GS // D_MODEL)),
            pl.BlockSpec((tc, LANES), lambda bi, ci: (row(bi, ci), OFF_MISC // LANES)),
            pl.BlockSpec((1, CONV_W - 1, CONV_DIM), lambda bi, ci: (bi, 0, 0)),
            pl.BlockSpec((1, N_SSD_HEADS, SSD_HEAD_DIM, D_STATE), lambda bi, ci: (bi, 0, 0, 0)),
            pl.BlockSpec((CONV_W, CONV_DIM), full2),
            pl.BlockSpec((1, CONV_DIM), full2),
            pl.BlockSpec((1, LANES), full2),
            pl.BlockSpec((1, LANES), full2),
            pl.BlockSpec((1, LANES), full2),
            pl.BlockSpec((1, D_INNER), full2),
            pl.BlockSpec((D_INNER, D_MODEL), full2),
        ],
        out_specs=[
            pl.BlockSpec((tc, D_MODEL), lambda bi, ci: (row(bi, ci), 0)),
            pl.BlockSpec((1, CONV_W - 1, CONV_DIM), lambda bi, ci: (bi, 0, 0)),
            pl.BlockSpec((1, N_SSD_HEADS, SSD_HEAD_DIM, D_STATE), lambda bi, ci: (bi, 0, 0, 0)),
        ],
        out_shape=[
            jax.ShapeDtypeStruct((b * l, D_MODEL), F32),
            jax.ShapeDtypeStruct((b, CONV_W - 1, CONV_DIM), F32),
            jax.ShapeDtypeStruct((b, N_SSD_HEADS, SSD_HEAD_DIM, D_STATE), F32),
        ],
        scratch_shapes=[
            pltpu.VMEM((8 + tc, CONV_DIM), F32),
            pltpu.VMEM((N_GROUPS, D_STATE, HEADS_PER_GROUP * SSD_HEAD_DIM), F32),
            pltpu.VMEM((tc, D_INNER), F32),
            pltpu.VMEM((tc, N_GROUPS * D_STATE), F32),
            pltpu.VMEM((tc, N_GROUPS * D_STATE), BF16),
            pltpu.VMEM((tc, D_INNER), F32),
        ],
        compiler_params=_cparams(("arbitrary", "arbitrary")),
        name="ssd",
    )(p, p, p, p, hist, h0, conv_w, conv_b, dtb, a_neg, dskip, ng, wo_b)


def _seg_mean64(xsq, bd):
    hi, lo = _split2(xsq)
    return (_dot(hi, bd) + _dot(lo, bd)) * (1.0 / 64.0)


def _prep_kernel(q_ref, k_ref, v_ref, qi_ref, misc_ref, qg_ref, kg_ref, ig_ref, bd_ref,
                 qn_ref, kn_ref, vv_ref, kin_ref, kb_ref, vb_ref, kib_ref, qib_ref, wi_ref):
    bd = bd_ref[...]
    w = 256
    for ct in range(D_ATT // w):
        cs = slice(ct * w, (ct + 1) * w)
        q = q_ref[:, cs]
        qn = q * lax.rsqrt(_seg_mean64(q * q, bd) + EPS) * qg_ref[:, cs]
        qn_ref[0, :, cs] = (qn * (HEAD_DIM ** -0.5 * LOG2E)).astype(BF16)
    k = k_ref[...]
    kn = k * lax.rsqrt(_seg_mean64(k * k, bd) + EPS) * kg_ref[...]
    kn_ref[0] = kn
    v = v_ref[...]
    vv_ref[0] = v
    knb = kn.astype(BF16)
    vb = v.astype(BF16)
    for g in range(N_KV):
        kb_ref[0, g] = knb[:, g * HEAD_DIM:(g + 1) * HEAD_DIM]
        vb_ref[0, g] = vb[:, g * HEAD_DIM:(g + 1) * HEAD_DIM]
    misc = misc_ref[...]
    mn = misc * lax.rsqrt(_seg_mean64(misc * misc, bd[:LANES, :LANES]) + EPS) * ig_ref[...]
    kin = mn[:, MISC_KI:MISC_KI + D_IDX]
    kin_ref[0] = kin
    kib_ref[0] = kin.astype(BF16)
    qib_ref[0] = (qi_ref[...] * (D_IDX ** -0.5)).astype(BF16)
    wi_ref[0] = misc[:, MISC_WI:MISC_WI + N_IDX_HEADS] * (N_IDX_HEADS ** -0.5)


def _prep(p, b, l, qg, kg, ig, bd):
    tm = min(256, l)
    nt = l // tm
    row = lambda bi, ti: bi * nt + ti
    kvw = N_KV * HEAD_DIM
    full2 = lambda bi, ti: (0, 0)
    tok3 = lambda bi, ti: (bi, ti, 0)
    return pl.pallas_call(
        _prep_kernel,
        grid=(b, nt),
        in_specs=[
            pl.BlockSpec((tm, D_ATT), lambda bi, ti: (row(bi, ti), OFF_Q // D_ATT)),
            pl.BlockSpec((tm, kvw), lambda bi, ti: (row(bi, ti), OFF_K // kvw)),
            pl.BlockSpec((tm, kvw), lambda bi, ti: (row(bi, ti), OFF_V // kvw)),
            pl.BlockSpec((tm, N_IDX_HEADS * D_IDX), lambda bi, ti: (row(bi, ti), OFF_QI // (N_IDX_HEADS * D_IDX))),
            pl.BlockSpec((tm, LANES), lambda bi, ti: (row(bi, ti), OFF_MISC // LANES)),
            pl.BlockSpec((1, D_ATT), full2),
            pl.BlockSpec((1, kvw), full2),
            pl.BlockSpec((1, LANES), full2),
            pl.BlockSpec((256, 256), full2),
        ],
        out_specs=[
            pl.BlockSpec((1, tm, D_ATT), tok3),
            pl.BlockSpec((1, tm, kvw), tok3),
            pl.BlockSpec((1, tm, kvw), tok3),
            pl.BlockSpec((1, tm, D_IDX), tok3),
            pl.BlockSpec((1, N_KV, tm, HEAD_DIM), lambda bi, ti: (bi, 0, ti, 0)),
            pl.BlockSpec((1, N_KV, tm, HEAD_DIM), lambda bi, ti: (bi, 0, ti, 0)),
            pl.BlockSpec((1, tm, D_IDX), tok3),
            pl.BlockSpec((1, tm, N_IDX_HEADS * D_IDX), tok3),
            pl.BlockSpec((1, tm, N_IDX_HEADS), tok3),
        ],
        out_shape=[
            jax.ShapeDtypeStruct((b, l, D_ATT), BF16),
            jax.ShapeDtypeStruct((b, l, kvw), F32),
            jax.ShapeDtypeStruct((b, l, kvw), F32),
            jax.ShapeDtypeStruct((b, l, D_IDX), F32),
            jax.ShapeDtypeStruct((b, N_KV, l, HEAD_DIM), BF16),
            jax.ShapeDtypeStruct((b, N_KV, l, HEAD_DIM), BF16),
            jax.ShapeDtypeStruct((b, l, D_IDX), BF16),
            jax.ShapeDtypeStruct((b, l, N_IDX_HEADS * D_IDX), BF16),
            jax.ShapeDtypeStruct((b, l, N_IDX_HEADS), F32),
        ],
        compiler_params=_cparams(("arbitrary", "arbitrary")),
        name="prep",
    )(p, p, p, p, p, qg, kg, ig, bd)


def _alibi_slope(h):
    return 2.0 ** (-8.0 * (h + 1) / N_HEADS)


def _dsa_kernel(qn_ref, qi_ref, wi_ref, k_ref, v_ref, ki_ref, o_ref,
                key_ref, qg_ref, qih_ref, wb_ref, m_ref, l_ref, acc_ref, sel_ref, slope_ref,
                *, tq, tk, past, n_sel, s_pad):
    i = pl.program_id(1)
    q0 = past + i * tq
    nkt = (q0 + tq + tk - 1) // tk
    nsub = tk // LANES
    qpos = q0 + lax.broadcasted_iota(jnp.int32, (tq, tk), 0)
    lane = lax.broadcasted_iota(jnp.int32, (tq, tk), 1)

    for h in range(N_IDX_HEADS):
        qih_ref[h] = qi_ref[0, :, h * D_IDX:(h + 1) * D_IDX]
        wb_ref[h] = jnp.broadcast_to(wi_ref[0, :, h:h + 1], (tq, LANES))
    for g in range(N_KV):
        for e in range(KV_GROUP):
            h = g * KV_GROUP + e
            qg_ref[g, e * tq:(e + 1) * tq, :] = qn_ref[0, :, h * HEAD_DIM:(h + 1) * HEAD_DIM]

    def score_body(kt, carry):
        ks = pl.multiple_of(kt * tk, tk)
        ki_t = ki_ref[0, pl.ds(ks, tk), :]
        acc = jnp.zeros((tq, tk), F32)
        for h in range(N_IDX_HEADS):
            s = _dot_nt(qih_ref[h], ki_t)
            wfull = jnp.concatenate([wb_ref[h]] * nsub, axis=1)
            acc = acc + wfull * jnp.maximum(s, 0.0)
        bits = lax.bitcast_convert_type(acc, jnp.int32)
        key = bits ^ ((bits >> 31) & 0x7FFFFFFF)
        adm = ((ks + lane) >> 6) <= (qpos >> 6)
        key_ref[:, pl.ds(ks, tk)] = jnp.where(adm, key, INT_MIN)
        return carry

    lax.fori_loop(0, nkt, score_body, 0)

    def count(pred):
        def body(kt, part):
            ks = pl.multiple_of(kt * tk, tk)
            hit = jnp.where(pred(key_ref[:, pl.ds(ks, tk)], ks + lane), 1.0, 0.0)
            for j in range(nsub):
                part = part + hit[:, j * LANES:(j + 1) * LANES]
            return part
        part = lax.fori_loop(0, nkt, body, jnp.zeros((tq, LANES), F32))
        return jnp.sum(part, axis=1, keepdims=True)

    def bit_body(bidx, t):
        cand = t | lax.shift_left(jnp.int32(1), 31 - bidx)
        cand_s = cand ^ INT_MIN
        cnt = count(lambda key, kidx: key >= cand_s)
        return jnp.where(cnt >= n_sel, cand, t)

    t_u = lax.fori_loop(0, 32, bit_body, jnp.zeros((tq, 1), jnp.int32))
    thr = t_u ^ INT_MIN

    cnt_gt = count(lambda key, kidx: key > thr)
    cnt_ge = count(lambda key, kidx: key >= thr)
    need = n_sel - cnt_gt
    short = t_u == 0
    sel_ref[:, 0:1] = jnp.where(short, 0, s_pad).astype(jnp.int32)
    excess = jnp.where((cnt_ge > n_sel) & jnp.logical_not(short), 1.0, 0.0)

    @pl.when(jnp.max(excess) > 0.0)
    def _():
        nbits = max(1, (s_pad - 1).bit_length())

        def idx_body(bidx, m):
            cand = m | lax.shift_left(jnp.int32(1), nbits - 1 - bidx)
            cnt = count(lambda key, kidx: (key == thr) & (kidx < cand))
            return jnp.where(cnt < need, cand, m)

        m = lax.fori_loop(0, nbits, idx_body, jnp.zeros((tq, 1), jnp.int32))
        sel_ref[:, 0:1] = jnp.where(short, 0, m + 1)

    jlim = sel_ref[:, 0:1]

    m_ref[...] = jnp.full(m_ref.shape, NEG_BIG, F32)
    l_ref[...] = jnp.zeros(l_ref.shape, F32)
    acc_ref[...] = jnp.zeros(acc_ref.shape, F32)

    for h in range(N_HEADS):
        slope_ref[h * tq:(h + 1) * tq, :] = jnp.full((tq, LANES), _alibi_slope(h) * LOG2E, F32)

    def att_body(kt, carry):
        ks = pl.multiple_of(kt * tk, tk)
        key = key_ref[:, pl.ds(ks, tk)]
        kidx = ks + lane
        sel = (key > thr) | ((key == thr) & (kidx < jlim))
        dist = jnp.abs(qpos - kidx).astype(F32)
        dsel = jnp.where(sel, dist, jnp.inf)
        rows = [slice(g * KV_GROUP * tq, (g + 1) * KV_GROUP * tq) for g in range(N_KV)]
        m_olds = [m_ref[r, :] for r in rows]
        l_olds = [l_ref[r, :] for r in rows]
        acc_olds = [acc_ref[r, :] for r in rows]
        new = []
        for g in range(N_KV):
            k_t = k_ref[0, g, pl.ds(ks, tk), :]
            v_t = v_ref[0, g, pl.ds(ks, tk), :]
            s = _dot_nt(qg_ref[g], k_t)
            sl = slope_ref[rows[g], :]
            lg = s - jnp.concatenate([sl] * nsub, axis=1) * jnp.concatenate([dsel] * KV_GROUP, axis=0)
            m_new = jnp.maximum(m_olds[g], jnp.max(lg, axis=1, keepdims=True))
            alpha = jnp.exp2(m_olds[g] - m_new)
            p = jnp.exp2(lg - m_new[:, 0:1])
            l_new = alpha * l_olds[g] + jnp.sum(p, axis=1, keepdims=True)
            acc_new = alpha[:, 0:HEAD_DIM] * acc_olds[g] + _dot(p.astype(BF16), v_t)
            new.append((m_new, l_new, acc_new))
        for g in range(N_KV):
            m_ref[rows[g], :], l_ref[rows[g], :], acc_ref[rows[g], :] = new[g]
        return carry

    lax.fori_loop(0, nkt, att_body, 0)

    for h in range(N_HEADS):
        r = slice(h * tq, (h + 1) * tq)
        o_ref[0, :, h * HEAD_DIM:(h + 1) * HEAD_DIM] = acc_ref[r, :] / l_ref[r, 0:HEAD_DIM]


def _dsa(qn, qib, wis, k_all, v_all, ki_all, past, n_sel):
    b, l, _ = qn.shape
    s_pad = k_all.shape[2]
    tq = min(128, l)
    tk = 512
    assert s_pad % tk == 0 and l % tq == 0 and past % CHUNK == 0 and tq % CHUNK == 0
    kern = functools.partial(_dsa_kernel, tq=tq, tk=tk, past=past, n_sel=n_sel, s_pad=s_pad)
    tok3 = lambda bi, qi: (bi, qi, 0)
    return pl.pallas_call(
        kern,
        grid=(b, l // tq),
        in_specs=[
            pl.BlockSpec((1, tq, D_ATT), tok3),
            pl.BlockSpec((1, tq, N_IDX_HEADS * D_IDX), tok3),
            pl.BlockSpec((1, tq, N_IDX_HEADS), tok3),
            pl.BlockSpec((1, N_KV, s_pad, HEAD_DIM), lambda bi, qi: (bi, 0, 0, 0)),
            pl.BlockSpec((1, N_KV, s_pad, HEAD_DIM), lambda bi, qi: (bi, 0, 0, 0)),
            pl.BlockSpec((1, s_pad, D_IDX), lambda bi, qi: (bi, 0, 0)),
        ],
        out_specs=pl.BlockSpec((1, tq, D_ATT), tok3),
        out_shape=jax.ShapeDtypeStruct((b, l, D_ATT), F32),
        scratch_shapes=[
            pltpu.VMEM((tq, s_pad), jnp.int32),
            pltpu.VMEM((N_KV, KV_GROUP * tq, HEAD_DIM), BF16),
            pltpu.VMEM((N_IDX_HEADS, tq, D_IDX), BF16),
            pltpu.VMEM((N_IDX_HEADS, tq, LANES), F32),
            pltpu.VMEM((N_HEADS * tq, LANES), F32),
            pltpu.VMEM((N_HEADS * tq, LANES), F32),
            pltpu.VMEM((N_HEADS * tq, HEAD_DIM), F32),
            pltpu.VMEM((tq, LANES), jnp.int32),
            pltpu.VMEM((N_HEADS * tq, LANES), F32),
        ],
        compiler_params=_cparams(("arbitrary", "arbitrary")),
        name="dsa",
    )(qn, qib, wis, k_all, v_all, ki_all)


def _merge_kernel(x_ref, oa_ref, za_ref, ga_ref, os_ref, wa_ref, wout_ref, y_ref):
    ya = (oa_ref[...] * _silu(za_ref[...])).astype(BF16)
    merged = os_ref[...] + _sigmoid(ga_ref[...]) * _dot(ya, wa_ref[...])
    y_ref[...] = x_ref[...] + _dot(merged.astype(BF16), wout_ref[...])


def _merge(x2d, oa2d, p, os2d, wa_b, wout_b):
    t = x2d.shape[0]
    tm = min(512, t)
    tok = lambda i: (i, 0)
    full = lambda i: (0, 0)
    return pl.pallas_call(
        _merge_kernel,
        grid=(t // tm,),
        in_specs=[
            pl.BlockSpec((tm, D_MODEL), tok),
            pl.BlockSpec((tm, D_ATT), tok),
            pl.BlockSpec((tm, D_ATT), lambda i: (i, OFF_ZA // D_ATT)),
            pl.BlockSpec((tm, D_MODEL), lambda i: (i, OFF_GA // D_MODEL)),
            pl.BlockSpec((tm, D_MODEL), tok),
            pl.BlockSpec((D_ATT, D_MODEL), full),
            pl.BlockSpec((D_MODEL, D_MODEL), full),
        ],
        out_specs=pl.BlockSpec((tm, D_MODEL), tok),
        out_shape=jax.ShapeDtypeStruct((t, D_MODEL), F32),
        compiler_params=_cparams(("arbitrary",)),
        name="merge",
    )(x2d, oa2d, p, p, os2d, wa_b, wout_b)


def _pad_lanes(v, n=LANES):
    v = v.reshape(1, -1).astype(F32)
    return jnp.pad(v, ((0, 0), (0, n - v.shape[1])))


def _layer_weights(norm_g, w_in, conv_w, conv_b, dt_bias, a_log, d_skip, ssd_norm_g, q_norm_g, k_norm_g,
                   idx_k_norm_g, w_ssd_o, w_att_o, w_out):
    sizes = (D_INNER, CONV_DIM, N_SSD_HEADS, D_ATT, N_KV * HEAD_DIM, N_KV * HEAD_DIM, D_ATT,
             N_IDX_HEADS * D_IDX, D_IDX, N_IDX_HEADS, D_MODEL, D_MODEL)
    parts, off = [], 0
    for n in sizes:
        parts.append(w_in[:, off:off + n])
        off += n
    z_s, xbc, dt, q, k, v, z_a, qi, ki, wi, g_s, g_a = parts
    pad = jnp.zeros((D_MODEL, LANES - N_SSD_HEADS - N_IDX_HEADS - D_IDX), w_in.dtype)
    w_perm = jnp.concatenate([xbc, g_s, z_s, q, z_a, g_a, qi, k, v, dt, wi, pad, ki], axis=1).astype(BF16)
    seg = jnp.arange(256) // 64
    return dict(
        norm_g=norm_g.reshape(1, D_MODEL),
        w_perm=w_perm,
        conv_w=conv_w,
        conv_b=conv_b.reshape(1, CONV_DIM),
        dtb=_pad_lanes(dt_bias),
        a_neg=_pad_lanes(-jnp.exp(a_log.astype(F32))),
        dskip=_pad_lanes(d_skip),
        ng=ssd_norm_g.reshape(1, D_INNER),
        wo_b=w_ssd_o.astype(BF16),
        qg=jnp.tile(q_norm_g.reshape(1, HEAD_DIM), (1, N_HEADS)),
        kg=jnp.tile(k_norm_g.reshape(1, HEAD_DIM), (1, N_KV)),
        ig=jnp.concatenate([jnp.ones((1, LANES - D_IDX), F32), idx_k_norm_g.reshape(1, D_IDX)], axis=1),
        bd=(seg[:, None] == seg[None, :]).astype(BF16),
        wa_b=w_att_o.astype(BF16),
        wout_b=w_out.astype(BF16),
    )


def _head_major_bf16(t):
    return jnp.transpose(t, (0, 2, 1, 3)).astype(BF16)


def _mixer_layer(x, hist, h0, k_past, v_past, ki_past, w):
    b, l, _ = x.shape
    x2d = x.reshape(b * l, D_MODEL)
    p = _proj(x2d, w["norm_g"], w["w_perm"])
    os2d, conv_new, ssm_new = _ssd(p, b, l, hist, h0, w["conv_w"], w["conv_b"], w["dtb"], w["a_neg"],
                                   w["dskip"], w["ng"], w["wo_b"])
    qn, kn, vv, kin, kb, vb, kib, qib, wis = _prep(p, b, l, w["qg"], w["kg"], w["ig"], w["bd"])
    past = 0 if k_past is None else k_past.shape[1]
    s = past + l
    s_pad = -(-s // 512) * 512
    if k_past is not None:
        kb = jnp.concatenate([_head_major_bf16(k_past), kb], axis=2)
        vb = jnp.concatenate([_head_major_bf16(v_past), vb], axis=2)
        kib = jnp.concatenate([ki_past.astype(BF16), kib], axis=1)
    if s_pad != s:
        kb = jnp.pad(kb, ((0, 0), (0, 0), (0, s_pad - s), (0, 0)))
        vb = jnp.pad(vb, ((0, 0), (0, 0), (0, s_pad - s), (0, 0)))
        kib = jnp.pad(kib, ((0, 0), (0, s_pad - s), (0, 0)))
    n_sel = min(TOPK_MAX, s // 4)
    oa = _dsa(qn, qib, wis, kb, vb, kib, past, n_sel)
    y2d = _merge(x2d, oa.reshape(b * l, D_ATT), p, os2d, w["wa_b"], w["wout_b"])
    return (y2d.reshape(b, l, D_MODEL), kn.reshape(b, l, N_KV, HEAD_DIM), vv.reshape(b, l, N_KV, HEAD_DIM),
            kin, conv_new, ssm_new)


def kernel(x_prompt, x_sample, cache_k, cache_v, cache_idx_k, state_conv, state_ssm, norm_g, w_in, conv_w,
           conv_b, dt_bias, a_log, d_skip, ssd_norm_g, q_norm_g, k_norm_g, idx_k_norm_g, w_ssd_o, w_att_o,
           w_out):
    depth = w_in.shape[0]
    yp, ys = x_prompt, x_sample
    b_p = x_prompt.shape[0]
    outs_p, outs_s = [], []
    for lyr in range(depth):
        w = _layer_weights(norm_g[lyr], w_in[lyr], conv_w[lyr], conv_b[lyr], dt_bias[lyr], a_log[lyr],
                           d_skip[lyr], ssd_norm_g[lyr], q_norm_g[lyr], k_norm_g[lyr], idx_k_norm_g[lyr],
                           w_ssd_o[lyr], w_att_o[lyr], w_out[lyr])
        conv0 = jnp.zeros((b_p, CONV_W - 1, CONV_DIM), F32)
        ssm0 = jnp.zeros((b_p, N_SSD_HEADS, SSD_HEAD_DIM, D_STATE), F32)
        yp, *rest_p = _mixer_layer(yp, conv0, ssm0, None, None, None, w)
        ys, *rest_s = _mixer_layer(ys, state_conv[lyr], state_ssm[lyr], cache_k[lyr], cache_v[lyr],
                                   cache_idx_k[lyr], w)
        outs_p.append(rest_p)
        outs_s.append(rest_s)
    stack = lambda outs, j: jnp.stack([o[j] for o in outs])
    return (yp, ys,
            stack(outs_p, 0), stack(outs_p, 1), stack(outs_p, 2), stack(outs_p, 3), stack(outs_p, 4),
            stack(outs_s, 0), stack(outs_s, 1), stack(outs_s, 2), stack(outs_s, 3), stack(outs_s, 4))
```

```python
import functools
import math

import jax
import jax.numpy as jnp
from jax import lax
from jax.experimental import pallas as pl
from jax.experimental.pallas import tpu as pltpu

D_MODEL = 1024
CHUNK = 64
D_INNER = 2048
SSD_HEAD_DIM = 64
N_SSD_HEADS = 32
N_GROUPS = 4
HEADS_PER_GROUP = N_SSD_HEADS // N_GROUPS
D_STATE = 128
CONV_W = 4
CONV_DIM = D_INNER + 2 * N_GROUPS * D_STATE
HEAD_DIM = 64
N_HEADS = 16
N_KV = 4
KV_GROUP = N_HEADS // N_KV
D_ATT = N_HEADS * HEAD_DIM
N_IDX_HEADS = 8
D_IDX = 64
TOPK_MAX = 256
EPS = 1e-6

OFF_XBC = 0
OFF_GS = 3072
OFF_ZS = 4096
OFF_Q = 6144
OFF_ZA = 7168
OFF_GA = 8192
OFF_QI = 9216
OFF_K = 9728
OFF_V = 9984
OFF_MISC = 10240
N_PROJ = 10368
MISC_DT = 0
MISC_WI = 32
MISC_KI = 64

LANES = 128
VMEM_LIMIT = 56 * 1024 * 1024
INT_MIN = -(2 ** 31)
NEG_BIG = -1e30
LOG2E = 1.4426950408889634

F32 = jnp.float32
BF16 = jnp.bfloat16


def _cparams(sem):
    return pltpu.CompilerParams(dimension_semantics=sem, vmem_limit_bytes=VMEM_LIMIT)


def _sigmoid(x):
    return 1.0 / (1.0 + jnp.exp(-x))


def _silu(x):
    return x * _sigmoid(x)


def _softplus(x):
    return jnp.maximum(x, 0.0) + jnp.log(1.0 + jnp.exp(-jnp.abs(x)))


def _split2(v):
    hi = v.astype(BF16)
    lo = (v - hi.astype(F32)).astype(BF16)
    return hi, lo


def _split3(v):
    hi = v.astype(BF16)
    r = v - hi.astype(F32)
    mid = r.astype(BF16)
    lo = (r - mid.astype(F32)).astype(BF16)
    return hi, mid, lo


def _dot(a, b):
    return jnp.dot(a, b, preferred_element_type=F32)


def _dot_nt(a, b):
    return lax.dot_general(a, b, (((1,), (1,)), ((), ())), preferred_element_type=F32)


def _proj_kernel(x_ref, g_ref, w_ref, o_ref):
    x = x_ref[...]
    ms = jnp.mean(x * x, axis=-1, keepdims=True)
    h = (x * lax.rsqrt(ms + EPS) * g_ref[...]).astype(BF16)
    o_ref[...] = _dot(h, w_ref[...])


def _proj(x2d, norm_g, w_perm):
    t = x2d.shape[0]
    tm = min(512, t)
    tn = 3456
    return pl.pallas_call(
        _proj_kernel,
        grid=(N_PROJ // tn, t // tm),
        in_specs=[
            pl.BlockSpec((tm, D_MODEL), lambda j, i: (i, 0)),
            pl.BlockSpec((1, D_MODEL), lambda j, i: (0, 0)),
            pl.BlockSpec((D_MODEL, tn), lambda j, i: (0, j)),
        ],
        out_specs=pl.BlockSpec((tm, tn), lambda j, i: (i, j)),
        out_shape=jax.ShapeDtypeStruct((t, N_PROJ), F32),
        compiler_params=_cparams(("arbitrary", "arbitrary")),
        name="proj",
    )(x2d, norm_g, w_perm)


def _ssd_kernel(xbc_ref, zs_ref, gs_ref, misc_ref, hist_ref, h0_ref, convw_ref, convb_ref, dtb_ref,
                a_ref, dskip_ref, ng_ref, wo_ref,
                os_ref, convn_ref, ssm_ref,
                pad_ref, ht_ref, xs_ref, b_ref, c_ref, y_ref, *, tc):
    c = pl.program_id(1)
    nc = pl.num_programs(1)
    hp = SSD_HEAD_DIM
    gw = HEADS_PER_GROUP * hp

    @pl.when(c == 0)
    def _():
        pad_ref[5:8, :] = hist_ref[0]
        for g in range(N_GROUPS):
            hg = h0_ref[0, g * HEADS_PER_GROUP:(g + 1) * HEADS_PER_GROUP].reshape(gw, D_STATE)
            ht_ref[g] = hg.T

    pad_ref[8:8 + tc, :] = xbc_ref[...]
    cw = 512
    for ct in range(CONV_DIM // cw):
        cs = slice(ct * cw, (ct + 1) * cw)
        acc = convb_ref[:, cs] + pad_ref[5:5 + tc, cs] * convw_ref[0:1, cs]
        acc = acc + pad_ref[6:6 + tc, cs] * convw_ref[1:2, cs]
        acc = acc + pad_ref[7:7 + tc, cs] * convw_ref[2:3, cs]
        acc = acc + pad_ref[8:8 + tc, cs] * convw_ref[3:4, cs]
        u = _silu(acc)
        if ct < D_INNER // cw:
            xs_ref[:, cs] = u
        elif ct == D_INNER // cw:
            b_ref[...] = u
        else:
            c_ref[...] = u.astype(BF16)
    last3 = pad_ref[8 + tc - 3:8 + tc, :]
    pad_ref[5:8, :] = last3

    @pl.when(c == nc - 1)
    def _():
        convn_ref[0] = last3

    dt = _softplus(misc_ref[...] + dtb_ref[...])
    la = dt * a_ref[...]
    row = lax.broadcasted_iota(jnp.int32, (tc, tc), 0)
    col = lax.broadcasted_iota(jnp.int32, (tc, tc), 1)
    tril = col <= row
    trilb = jnp.where(tril, 1.0, 0.0).astype(BF16)
    acum = None
    for part in _split3(la):
        d = _dot(trilb, part)
        acum = d if acum is None else acum + d
    acum_t = acum.T
    a_last = acum[tc - 1:tc, :]
    ea = jnp.exp(acum)
    dend = jnp.exp(a_last - acum)
    atot = jnp.exp(a_last)

    for g in range(N_GROUPS):
        bg = b_ref[:, g * D_STATE:(g + 1) * D_STATE]
        cg = c_ref[:, g * D_STATE:(g + 1) * D_STATE]
        bgb = bg.astype(BF16)
        cb = _dot_nt(cg, bgb)
        h_old = ht_ref[g]
        yoff = _dot(cg, h_old.astype(BF16))
        dxs = []
        decs = []
        for e in range(HEADS_PER_GROUP):
            h = g * HEADS_PER_GROUP + e
            hs = slice(h * hp, (h + 1) * hp)
            xs_h = xs_ref[:, hs]
            xk = xs_h * dt[:, h:h + 1]
            seg = acum[:, h:h + 1] - acum_t[h:h + 1, :]
            lmat = jnp.exp(jnp.where(tril, seg, -jnp.inf))
            w = (cb * lmat).astype(BF16)
            yd = _dot(w, xk.astype(BF16))
            y_h = yd + yoff[:, e * hp:(e + 1) * hp] * ea[:, h:h + 1] + dskip_ref[:, h:h + 1] * xs_h
            y_ref[:, hs] = y_h
            dxs.append((xk * dend[:, h:h + 1]).astype(BF16))
            decs.append(jnp.broadcast_to(atot[:, h:h + 1], (1, hp)))
        dx = jnp.concatenate(dxs, axis=1)
        dec = jnp.concatenate(decs, axis=1)
        st = _dot(bg.T.astype(BF16), dx)
        ht_ref[g] = dec * h_old + st

    out = None
    for g in range(N_GROUPS):
        gs_ = slice(g * gw, (g + 1) * gw)
        yg = y_ref[:, gs_] * _silu(zs_ref[:, gs_])
        ms = jnp.mean(yg * yg, axis=-1, keepdims=True)
        yn = (yg * lax.rsqrt(ms + EPS) * ng_ref[:, gs_]).astype(BF16)
        d = _dot(yn, wo_ref[gs_, :])
        out = d if out is None else out + d
    os_ref[...] = _sigmoid(gs_ref[...]) * out

    @pl.when(c == nc - 1)
    def _():
        for g in range(N_GROUPS):
            ssm_ref[0, g * HEADS_PER_GROUP:(g + 1) * HEADS_PER_GROUP] = (
                ht_ref[g].T.reshape(HEADS_PER_GROUP, hp, D_STATE))


def _ssd(p, b, l, hist, h0, conv_w, conv_b, dtb, a_neg, dskip, ng, wo_b):
    tc = min(256, l)
    nc = l // tc
    row = lambda bi, ci: bi * nc + ci
    kern = functools.partial(_ssd_kernel, tc=tc)
    full2 = lambda bi, ci: (0, 0)
    return pl.pallas_call(
        kern,
        grid=(b, nc),
        in_specs=[
            pl.BlockSpec((tc, CONV_DIM), lambda bi, ci: (row(bi, ci), OFF_XBC // CONV_DIM)),
            pl.BlockSpec((tc, D_INNER), lambda bi, ci: (row(bi, ci), OFF_ZS // D_INNER)),
            pl.BlockSpec((tc, D_MODEL), lambda bi, ci: (row(bi, ci), OFF_GS // D_MODEL)),
            pl.BlockSpec((tc, LANES), lambda bi, ci: (row(bi, ci), OFF_MISC // LANES)),
            pl.BlockSpec((1, CONV_W - 1, CONV_DIM), lambda bi, ci: (bi, 0, 0)),
            pl.BlockSpec((1, N_SSD_HEADS, SSD_HEAD_DIM, D_STATE), lambda bi, ci: (bi, 0, 0, 0)),
            pl.BlockSpec((CONV_W, CONV_DIM), full2),
            pl.BlockSpec((1, CONV_DIM), full2),
            pl.BlockSpec((1, LANES), full2),
            pl.BlockSpec((1, LANES), full2),
            pl.BlockSpec((1, LANES), full2),
            pl.BlockSpec((1, D_INNER), full2),
            pl.BlockSpec((D_INNER, D_MODEL), full2),
        ],
        out_specs=[
            pl.BlockSpec((tc, D_MODEL), lambda bi, ci: (row(bi, ci), 0)),
            pl.BlockSpec((1, CONV_W - 1, CONV_DIM), lambda bi, ci: (bi, 0, 0)),
            pl.BlockSpec((1, N_SSD_HEADS, SSD_HEAD_DIM, D_STATE), lambda bi, ci: (bi, 0, 0, 0)),
        ],
        out_shape=[
            jax.ShapeDtypeStruct((b * l, D_MODEL), F32),
            jax.ShapeDtypeStruct((b, CONV_W - 1, CONV_DIM), F32),
            jax.ShapeDtypeStruct((b, N_SSD_HEADS, SSD_HEAD_DIM, D_STATE), F32),
        ],
        scratch_shapes=[
            pltpu.VMEM((8 + tc, CONV_DIM), F32),
            pltpu.VMEM((N_GROUPS, D_STATE, HEADS_PER_GROUP * SSD_HEAD_DIM), F32),
            pltpu.VMEM((tc, D_INNER), F32),
            pltpu.VMEM((tc, N_GROUPS * D_STATE), F32),
            pltpu.VMEM((tc, N_GROUPS * D_STATE), BF16),
            pltpu.VMEM((tc, D_INNER), F32),
        ],
        compiler_params=_cparams(("arbitrary", "arbitrary")),
        name="ssd",
    )(p, p, p, p, hist, h0, conv_w, conv_b, dtb, a_neg, dskip, ng, wo_b)


def _seg_mean64(xsq, bd):
    hi, lo = _split2(xsq)
    return (_dot(hi, bd) + _dot(lo, bd)) * (1.0 / 64.0)


def _prep_kernel(q_ref, k_ref, v_ref, qi_ref, misc_ref, qg_ref, kg_ref, ig_ref, bd_ref,
                 qn_ref, kn_ref, vv_ref, kin_ref, kb_ref, vb_ref, kib_ref, qib_ref, wi_ref):
    bd = bd_ref[...]
    w = 256
    for ct in range(D_ATT // w):
        cs = slice(ct * w, (ct + 1) * w)
        q = q_ref[:, cs]
        qn = q * lax.rsqrt(_seg_mean64(q * q, bd) + EPS) * qg_ref[:, cs]
        qn_ref[0, :, cs] = (qn * (HEAD_DIM ** -0.5 * LOG2E)).astype(BF16)
    k = k_ref[...]
    kn = k * lax.rsqrt(_seg_mean64(k * k, bd) + EPS) * kg_ref[...]
    kn_ref[0] = kn
    v = v_ref[...]
    vv_ref[0] = v
    knb = kn.astype(BF16)
    vb = v.astype(BF16)
    for g in range(N_KV):
        kb_ref[0, g] = knb[:, g * HEAD_DIM:(g + 1) * HEAD_DIM]
        vb_ref[0, g] = vb[:, g * HEAD_DIM:(g + 1) * HEAD_DIM]
    misc = misc_ref[...]
    mn = misc * lax.rsqrt(_seg_mean64(misc * misc, bd[:LANES, :LANES]) + EPS) * ig_ref[...]
    kin = mn[:, MISC_KI:MISC_KI + D_IDX]
    kin_ref[0] = kin
    kib_ref[0] = kin.astype(BF16)
    qib_ref[0] = (qi_ref[...] * (D_IDX ** -0.5)).astype(BF16)
    wi_ref[0] = misc[:, MISC_WI:MISC_WI + N_IDX_HEADS] * (N_IDX_HEADS ** -0.5)


def _prep(p, b, l, qg, kg, ig, bd):
    tm = min(256, l)
    nt = l // tm
    row = lambda bi, ti: bi * nt + ti
    kvw = N_KV * HEAD_DIM
    full2 = lambda bi, ti: (0, 0)
    tok3 = lambda bi, ti: (bi, ti, 0)
    return pl.pallas_call(
        _prep_kernel,
        grid=(b, nt),
        in_specs=[
            pl.BlockSpec((tm, D_ATT), lambda bi, ti: (row(bi, ti), OFF_Q // D_ATT)),
            pl.BlockSpec((tm, kvw), lambda bi, ti: (row(bi, ti), OFF_K // kvw)),
            pl.BlockSpec((tm, kvw), lambda bi, ti: (row(bi, ti), OFF_V // kvw)),
            pl.BlockSpec((tm, N_IDX_HEADS * D_IDX), lambda bi, ti: (row(bi, ti), OFF_QI // (N_IDX_HEADS * D_IDX))),
            pl.BlockSpec((tm, LANES), lambda bi, ti: (row(bi, ti), OFF_MISC // LANES)),
            pl.BlockSpec((1, D_ATT), full2),
            pl.BlockSpec((1, kvw), full2),
            pl.BlockSpec((1, LANES), full2),
            pl.BlockSpec((256, 256), full2),
        ],
        out_specs=[
            pl.BlockSpec((1, tm, D_ATT), tok3),
            pl.BlockSpec((1, tm, kvw), tok3),
            pl.BlockSpec((1, tm, kvw), tok3),
            pl.BlockSpec((1, tm, D_IDX), tok3),
            pl.BlockSpec((1, N_KV, tm, HEAD_DIM), lambda bi, ti: (bi, 0, ti, 0)),
            pl.BlockSpec((1, N_KV, tm, HEAD_DIM), lambda bi, ti: (bi, 0, ti, 0)),
            pl.BlockSpec((1, tm, D_IDX), tok3),
            pl.BlockSpec((1, tm, N_IDX_HEADS * D_IDX), tok3),
            pl.BlockSpec((1, tm, N_IDX_HEADS), tok3),
        ],
        out_shape=[
            jax.ShapeDtypeStruct((b, l, D_ATT), BF16),
            jax.ShapeDtypeStruct((b, l, kvw), F32),
            jax.ShapeDtypeStruct((b, l, kvw), F32),
            jax.ShapeDtypeStruct((b, l, D_IDX), F32),
            jax.ShapeDtypeStruct((b, N_KV, l, HEAD_DIM), BF16),
            jax.ShapeDtypeStruct((b, N_KV, l, HEAD_DIM), BF16),
            jax.ShapeDtypeStruct((b, l, D_IDX), BF16),
            jax.ShapeDtypeStruct((b, l, N_IDX_HEADS * D_IDX), BF16),
            jax.ShapeDtypeStruct((b, l, N_IDX_HEADS), F32),
        ],
        compiler_params=_cparams(("arbitrary", "arbitrary")),
        name="prep",
    )(p, p, p, p, p, qg, kg, ig, bd)


def _alibi_slope(h):
    return 2.0 ** (-8.0 * (h + 1) / N_HEADS)


def _dsa_kernel(qn_ref, qi_ref, wi_ref, k_ref, v_ref, ki_ref, o_ref,
                key_ref, qg_ref, qih_ref, wb_ref, m_ref, l_ref, acc_ref, sel_ref, slope_ref,
                *, tq, tk, past, n_sel, s_pad):
    i = pl.program_id(1)
    q0 = past + i * tq
    nkt = (q0 + tq + tk - 1) // tk
    nsub = tk // LANES
    qpos = q0 + lax.broadcasted_iota(jnp.int32, (tq, tk), 0)
    lane = lax.broadcasted_iota(jnp.int32, (tq, tk), 1)

    for h in range(N_IDX_HEADS):
        qih_ref[h] = qi_ref[0, :, h * D_IDX:(h + 1) * D_IDX]
        wb_ref[h] = jnp.broadcast_to(wi_ref[0, :, h:h + 1], (tq, LANES))
    for g in range(N_KV):
        for e in range(KV_GROUP):
            h = g * KV_GROUP + e
            qg_ref[g, e * tq:(e + 1) * tq, :] = qn_ref[0, :, h * HEAD_DIM:(h + 1) * HEAD_DIM]

    def score_body(kt, carry):
        ks = pl.multiple_of(kt * tk, tk)
        ki_t = ki_ref[0, pl.ds(ks, tk), :]
        acc = jnp.zeros((tq, tk), F32)
        for h in range(N_IDX_HEADS):
            s = _dot_nt(qih_ref[h], ki_t)
            wfull = jnp.concatenate([wb_ref[h]] * nsub, axis=1)
            acc = acc + wfull * jnp.maximum(s, 0.0)
        bits = lax.bitcast_convert_type(acc, jnp.int32)
        key = bits ^ ((bits >> 31) & 0x7FFFFFFF)
        adm = ((ks + lane) >> 6) <= (qpos >> 6)
        key_ref[:, pl.ds(ks, tk)] = jnp.where(adm, key, INT_MIN)
        return carry

    lax.fori_loop(0, nkt, score_body, 0)

    def count(pred):
        def body(kt, part):
            ks = pl.multiple_of(kt * tk, tk)
            hit = jnp.where(pred(key_ref[:, pl.ds(ks, tk)], ks + lane), 1.0, 0.0)
            for j in range(nsub):
                part = part + hit[:, j * LANES:(j + 1) * LANES]
            return part
        part = lax.fori_loop(0, nkt, body, jnp.zeros((tq, LANES), F32))
        return jnp.sum(part, axis=1, keepdims=True)

    def bit_body(bidx, t):
        cand = t | lax.shift_left(jnp.int32(1), 31 - bidx)
        cand_s = cand ^ INT_MIN
        cnt = count(lambda key, kidx: key >= cand_s)
        return jnp.where(cnt >= n_sel, cand, t)

    t_u = lax.fori_loop(0, 32, bit_body, jnp.zeros((tq, 1), jnp.int32))
    thr = t_u ^ INT_MIN

    cnt_gt = count(lambda key, kidx: key > thr)
    cnt_ge = count(lambda key, kidx: key >= thr)
    need = n_sel - cnt_gt
    short = t_u == 0
    sel_ref[:, 0:1] = jnp.where(short, 0, s_pad).astype(jnp.int32)
    excess = jnp.where((cnt_ge > n_sel) & jnp.logical_not(short), 1.0, 0.0)

    @pl.when(jnp.max(excess) > 0.0)
    def _():
        nbits = max(1, (s_pad - 1).bit_length())

        def idx_body(bidx, m):
            cand = m | lax.shift_left(jnp.int32(1), nbits - 1 - bidx)
            cnt = count(lambda key, kidx: (key == thr) & (kidx < cand))
            return jnp.where(cnt < need, cand, m)

        m = lax.fori_loop(0, nbits, idx_body, jnp.zeros((tq, 1), jnp.int32))
        sel_ref[:, 0:1] = jnp.where(short, 0, m + 1)

    jlim = sel_ref[:, 0:1]

    m_ref[...] = jnp.full(m_ref.shape, NEG_BIG, F32)
    l_ref[...] = jnp.zeros(l_ref.shape, F32)
    acc_ref[...] = jnp.zeros(acc_ref.shape, F32)

    for h in range(N_HEADS):
        slope_ref[h * tq:(h + 1) * tq, :] = jnp.full((tq, LANES), _alibi_slope(h) * LOG2E, F32)

    def att_body(kt, carry):
        ks = pl.multiple_of(kt * tk, tk)
        key = key_ref[:, pl.ds(ks, tk)]
        kidx = ks + lane
        sel = (key > thr) | ((key == thr) & (kidx < jlim))
        dist = jnp.abs(qpos - kidx).astype(F32)
        dsel = jnp.where(sel, dist, jnp.inf)
        rows = [slice(g * KV_GROUP * tq, (g + 1) * KV_GROUP * tq) for g in range(N_KV)]
        m_olds = [m_ref[r, :] for r in rows]
        l_olds = [l_ref[r, :] for r in rows]
        acc_olds = [acc_ref[r, :] for r in rows]
        new = []
        for g in range(N_KV):
            k_t = k_ref[0, g, pl.ds(ks, tk), :]
            v_t = v_ref[0, g, pl.ds(ks, tk), :]
            s = _dot_nt(qg_ref[g], k_t)
            sl = slope_ref[rows[g], :]
            lg = s - jnp.concatenate([sl] * nsub, axis=1) * jnp.concatenate([dsel] * KV_GROUP, axis=0)
            m_new = jnp.maximum(m_olds[g], jnp.max(lg, axis=1, keepdims=True))
            alpha = jnp.exp2(m_olds[g] - m_new)
            p = jnp.exp2(lg - m_new[:, 0:1])
            l_new = alpha * l_olds[g] + jnp.sum(p, axis=1, keepdims=True)
            acc_new = alpha[:, 0:HEAD_DIM] * acc_olds[g] + _dot(p.astype(BF16), v_t)
            new.append((m_new, l_new, acc_new))
        for g in range(N_KV):
            m_ref[rows[g], :], l_ref[rows[g], :], acc_ref[rows[g], :] = new[g]
        return carry

    lax.fori_loop(0, nkt, att_body, 0)

    for h in range(N_HEADS):
        r = slice(h * tq, (h + 1) * tq)
        o_ref[0, :, h * HEAD_DIM:(h + 1) * HEAD_DIM] = acc_ref[r, :] / l_ref[r, 0:HEAD_DIM]


def _dsa(qn, qib, wis, k_all, v_all, ki_all, past, n_sel):
    b, l, _ = qn.shape
    s_pad = k_all.shape[2]
    tq = min(128, l)
    tk = 512
    assert s_pad % tk == 0 and l % tq == 0 and past % CHUNK == 0 and tq % CHUNK == 0
    kern = functools.partial(_dsa_kernel, tq=tq, tk=tk, past=past, n_sel=n_sel, s_pad=s_pad)
    tok3 = lambda bi, qi: (bi, qi, 0)
    return pl.pallas_call(
        kern,
        grid=(b, l // tq),
        in_specs=[
            pl.BlockSpec((1, tq, D_ATT), tok3),
            pl.BlockSpec((1, tq, N_IDX_HEADS * D_IDX), tok3),
            pl.BlockSpec((1, tq, N_IDX_HEADS), tok3),
            pl.BlockSpec((1, N_KV, s_pad, HEAD_DIM), lambda bi, qi: (bi, 0, 0, 0)),
            pl.BlockSpec((1, N_KV, s_pad, HEAD_DIM), lambda bi, qi: (bi, 0, 0, 0)),
            pl.BlockSpec((1, s_pad, D_IDX), lambda bi, qi: (bi, 0, 0)),
        ],
        out_specs=pl.BlockSpec((1, tq, D_ATT), tok3),
        out_shape=jax.ShapeDtypeStruct((b, l, D_ATT), F32),
        scratch_shapes=[
            pltpu.VMEM((tq, s_pad), jnp.int32),
            pltpu.VMEM((N_KV, KV_GROUP * tq, HEAD_DIM), BF16),
            pltpu.VMEM((N_IDX_HEADS, tq, D_IDX), BF16),
            pltpu.VMEM((N_IDX_HEADS, tq, LANES), F32),
            pltpu.VMEM((N_HEADS * tq, LANES), F32),
            pltpu.VMEM((N_HEADS * tq, LANES), F32),
            pltpu.VMEM((N_HEADS * tq, HEAD_DIM), F32),
            pltpu.VMEM((tq, LANES), jnp.int32),
            pltpu.VMEM((N_HEADS * tq, LANES), F32),
        ],
        compiler_params=_cparams(("arbitrary", "arbitrary")),
        name="dsa",
    )(qn, qib, wis, k_all, v_all, ki_all)


def _merge_kernel(x_ref, oa_ref, za_ref, ga_ref, os_ref, wa_ref, wout_ref, y_ref):
    ya = (oa_ref[...] * _silu(za_ref[...])).astype(BF16)
    merged = os_ref[...] + _sigmoid(ga_ref[...]) * _dot(ya, wa_ref[...])
    y_ref[...] = x_ref[...] + _dot(merged.astype(BF16), wout_ref[...])


def _merge(x2d, oa2d, p, os2d, wa_b, wout_b):
    t = x2d.shape[0]
    tm = min(512, t)
    tok = lambda i: (i, 0)
    full = lambda i: (0, 0)
    return pl.pallas_call(
        _merge_kernel,
        grid=(t // tm,),
        in_specs=[
            pl.BlockSpec((tm, D_MODEL), tok),
            pl.BlockSpec((tm, D_ATT), tok),
            pl.BlockSpec((tm, D_ATT), lambda i: (i, OFF_ZA // D_ATT)),
            pl.BlockSpec((tm, D_MODEL), lambda i: (i, OFF_GA // D_MODEL)),
            pl.BlockSpec((tm, D_MODEL), tok),
            pl.BlockSpec((D_ATT, D_MODEL), full),
            pl.BlockSpec((D_MODEL, D_MODEL), full),
        ],
        out_specs=pl.BlockSpec((tm, D_MODEL), tok),
        out_shape=jax.ShapeDtypeStruct((t, D_MODEL), F32),
        compiler_params=_cparams(("arbitrary",)),
        name="merge",
    )(x2d, oa2d, p, p, os2d, wa_b, wout_b)


def _pad_lanes(v, n=LANES):
    v = v.reshape(1, -1).astype(F32)
    return jnp.pad(v, ((0, 0), (0, n - v.shape[1])))


def _layer_weights(norm_g, w_in, conv_w, conv_b, dt_bias, a_log, d_skip, ssd_norm_g, q_norm_g, k_norm_g,
                   idx_k_norm_g, w_ssd_o, w_att_o, w_out):
    sizes = (D_INNER, CONV_DIM, N_SSD_HEADS, D_ATT, N_KV * HEAD_DIM, N_KV * HEAD_DIM, D_ATT,
             N_IDX_HEADS * D_IDX, D_IDX, N_IDX_HEADS, D_MODEL, D_MODEL)
    parts, off = [], 0
    for n in sizes:
        parts.append(w_in[:, off:off + n])
        off += n
    z_s, xbc, dt, q, k, v, z_a, qi, ki, wi, g_s, g_a = parts
    pad = jnp.zeros((D_MODEL, LANES - N_SSD_HEADS - N_IDX_HEADS - D_IDX), w_in.dtype)
    w_perm = jnp.concatenate([xbc, g_s, z_s, q, z_a, g_a, qi, k, v, dt, wi, pad, ki], axis=1).astype(BF16)
    seg = jnp.arange(256) // 64
    return dict(
        norm_g=norm_g.reshape(1, D_MODEL),
        w_perm=w_perm,
        conv_w=conv_w,
        conv_b=conv_b.reshape(1, CONV_DIM),
        dtb=_pad_lanes(dt_bias),
        a_neg=_pad_lanes(-jnp.exp(a_log.astype(F32))),
        dskip=_pad_lanes(d_skip),
        ng=ssd_norm_g.reshape(1, D_INNER),
        wo_b=w_ssd_o.astype(BF16),
        qg=jnp.tile(q_norm_g.reshape(1, HEAD_DIM), (1, N_HEADS)),
        kg=jnp.tile(k_norm_g.reshape(1, HEAD_DIM), (1, N_KV)),
        ig=jnp.concatenate([jnp.ones((1, LANES - D_IDX), F32), idx_k_norm_g.reshape(1, D_IDX)], axis=1),
        bd=(seg[:, None] == seg[None, :]).astype(BF16),
        wa_b=w_att_o.astype(BF16),
        wout_b=w_out.astype(BF16),
    )


def _head_major_bf16(t):
    return jnp.transpose(t, (0, 2, 1, 3)).astype(BF16)


def _mixer_layer(x, hist, h0, k_past, v_past, ki_past, w):
    b, l, _ = x.shape
    x2d = x.reshape(b * l, D_MODEL)
    p = _proj(x2d, w["norm_g"], w["w_perm"])
    os2d, conv_new, ssm_new = _ssd(p, b, l, hist, h0, w["conv_w"], w["conv_b"], w["dtb"], w["a_neg"],
                                   w["dskip"], w["ng"], w["wo_b"])
    qn, kn, vv, kin, kb, vb, kib, qib, wis = _prep(p, b, l, w["qg"], w["kg"], w["ig"], w["bd"])
    past = 0 if k_past is None else k_past.shape[1]
    s = past + l
    s_pad = -(-s // 512) * 512
    if k_past is not None:
        kb = jnp.concatenate([_head_major_bf16(k_past), kb], axis=2)
        vb = jnp.concatenate([_head_major_bf16(v_past), vb], axis=2)
        kib = jnp.concatenate([ki_past.astype(BF16), kib], axis=1)
    if s_pad != s:
        kb = jnp.pad(kb, ((0, 0), (0, 0), (0, s_pad - s), (0, 0)))
        vb = jnp.pad(vb, ((0, 0), (0, 0), (0, s_pad - s), (0, 0)))
        kib = jnp.pad(kib, ((0, 0), (0, s_pad - s), (0, 0)))
    n_sel = min(TOPK_MAX, s // 4)
    oa = _dsa(qn, qib, wis, kb, vb, kib, past, n_sel)
    y2d = _merge(x2d, oa.reshape(b * l, D_ATT), p, os2d, w["wa_b"], w["wout_b"])
    return (y2d.reshape(b, l, D_MODEL), kn.reshape(b, l, N_KV, HEAD_DIM), vv.reshape(b, l, N_KV, HEAD_DIM),
            kin, conv_new, ssm_new)


def kernel(x_prompt, x_sample, cache_k, cache_v, cache_idx_k, state_conv, state_ssm, norm_g, w_in, conv_w,
           conv_b, dt_bias, a_log, d_skip, ssd_norm_g, q_norm_g, k_norm_g, idx_k_norm_g, w_ssd_o, w_att_o,
           w_out):
    depth = w_in.shape[0]
    yp, ys = x_prompt, x_sample
    b_p = x_prompt.shape[0]
    outs_p, outs_s = [], []
    for lyr in range(depth):
        w = _layer_weights(norm_g[lyr], w_in[lyr], conv_w[lyr], conv_b[lyr], dt_bias[lyr], a_log[lyr],
                           d_skip[lyr], ssd_norm_g[lyr], q_norm_g[lyr], k_norm_g[lyr], idx_k_norm_g[lyr],
                           w_ssd_o[lyr], w_att_o[lyr], w_out[lyr])
        conv0 = jnp.zeros((b_p, CONV_W - 1, CONV_DIM), F32)
        ssm0 = jnp.zeros((b_p, N_SSD_HEADS, SSD_HEAD_DIM, D_STATE), F32)
        yp, *rest_p = _mixer_layer(yp, conv0, ssm0, None, None, None, w)
        ys, *rest_s = _mixer_layer(ys, state_conv[lyr], state_ssm[lyr], cache_k[lyr], cache_v[lyr],
                                   cache_idx_k[lyr], w)
        outs_p.append(rest_p)
        outs_s.append(rest_s)
    stack = lambda outs, j: jnp.stack([o[j] for o in outs])
    return (yp, ys,
            stack(outs_p, 0), stack(outs_p, 1), stack(outs_p, 2), stack(outs_p, 3), stack(outs_p, 4),
            stack(outs_s, 0), stack(outs_s, 1), stack(outs_s, 2), stack(outs_s, 3), stack(outs_s, 4))
```
